```python
import math
import jax, jax.numpy as jnp
from jax import lax
import numpy as np

D_MODEL = 1024
BATCH = 4
SEQ = 8192
DEPTH = 4

CHUNK = 64
N_A_LAYERS = DEPTH // 2
N_B_LAYERS = DEPTH - N_A_LAYERS

A_HEADS = 8
A_HEAD_K = 128
A_HEAD_V = 128
A_QK_WIDTH = A_HEADS * A_HEAD_K
A_V_WIDTH = A_HEADS * A_HEAD_V
A_CONV = 4
A_CONV_WIDTH = 2 * A_QK_WIDTH + A_V_WIDTH
A_IN_WIDTH = 2 * A_QK_WIDTH + 2 * A_V_WIDTH + 2 * A_HEADS

B_HEADS = 16
B_HEAD_DIM = 64
B_WIDTH = B_HEADS * B_HEAD_DIM
LEFT_CHUNKS = 8
BAND = (LEFT_CHUNKS + 1) * CHUNK
REL_CLIP = 256

FFN_DIM = 2816
FFN_CONV = 3

EPS = 1e-6
NEG_INF = -1e30

kernel_name = "yoco_gdn_chunkattn_convffn"


def rmsnorm(x, g):
    xf = x.astype(jnp.float32)
    y = xf * lax.rsqrt(jnp.mean(xf * xf, axis=-1, keepdims=True) + EPS)
    return (y * g.astype(jnp.float32)).astype(x.dtype)


def causal_dwconv(x, w):
    width = w.shape[0]
    return lax.conv_general_dilated(
        x, w[:, None, :].astype(x.dtype), window_strides=(1,), padding=[(width - 1, 0)],
        dimension_numbers=("NWC", "WIO", "NWC"), feature_group_count=x.shape[-1])


def _l2norm(t):
    return t * lax.rsqrt(jnp.sum(t * t, axis=-1, keepdims=True) + EPS)


def chunk_gated_delta_rule(q, k, v, beta, g):
    bsz, seq, nh, dk = q.shape
    dv = v.shape[-1]
    nc = seq // CHUNK

    def chunks(t):
        return t.reshape(bsz, nc, CHUNK, nh, -1).transpose(0, 3, 1, 2, 4)

    q, k, v = chunks(q), chunks(k), chunks(v)
    beta = chunks(beta[..., None])[..., 0]
    gcum = jnp.cumsum(chunks(g[..., None])[..., 0], axis=-1)

    causal = jnp.tril(jnp.ones((CHUNK, CHUNK), dtype=bool))
    strict = jnp.tril(jnp.ones((CHUNK, CHUNK), dtype=bool), k=-1)
    diff = gcum[..., :, None] - gcum[..., None, :]
    decay = jnp.where(causal, jnp.exp(jnp.where(causal, diff, 0.0)), 0.0)

    k_beta = k * beta[..., None]
    m = jnp.where(strict, jnp.einsum("bhnid,bhnjd->bhnij", k_beta, k) * decay, 0.0)
    eye = jnp.eye(CHUNK, dtype=m.dtype)
    rhs = jnp.concatenate([v * beta[..., None], k_beta * jnp.exp(gcum)[..., None]], axis=-1)
    uw = lax.linalg.triangular_solve(m + eye, rhs, left_side=True, lower=True, unit_diagonal=True)
    u, w = uw[..., :dv], uw[..., dv:]

    attn_qk = jnp.einsum("bhnid,bhnjd->bhnij", q, k) * decay
    q_dec = q * jnp.exp(gcum)[..., None]
    k_end = k * jnp.exp(gcum[..., -1:] - gcum)[..., None]
    chunk_decay = jnp.exp(gcum[..., -1])

    xs = tuple(jnp.moveaxis(t, 2, 0) for t in (q_dec, k_end, u, w, attn_qk, chunk_decay))

    def step(state, inp):
        qd, ke, u_c, w_c, a_c, dec = inp
        v_new = u_c - jnp.einsum("bhcd,bhdv->bhcv", w_c, state)
        o_c = jnp.einsum("bhcd,bhdv->bhcv", qd, state) + jnp.einsum("bhcj,bhjv->bhcv", a_c, v_new)
        state = state * dec[..., None, None] + jnp.einsum("bhcd,bhcv->bhdv", ke, v_new)
        return state, o_c

    state0 = jnp.zeros((bsz, nh, dk, dv), jnp.float32)
    _, o = lax.scan(step, state0, xs)
    return o.transpose(1, 0, 3, 2, 4).reshape(bsz, seq, nh, dv)


def gated_deltanet(xn, w_in, conv_w, A_log, dt_bias, out_norm_w, w_out):
    bsz, seq, _ = xn.shape
    proj = xn @ w_in
    qkv = jax.nn.silu(causal_dwconv(proj[..., :A_CONV_WIDTH], conv_w))
    z = proj[..., A_CONV_WIDTH:A_CONV_WIDTH + A_V_WIDTH]
    b_raw = proj[..., A_CONV_WIDTH + A_V_WIDTH:A_CONV_WIDTH + A_V_WIDTH + A_HEADS]
    a_raw = proj[..., A_CONV_WIDTH + A_V_WIDTH + A_HEADS:]
    f32 = jnp.float32
    q = _l2norm(qkv[..., :A_QK_WIDTH].reshape(bsz, seq, A_HEADS, A_HEAD_K).astype(f32)) * (A_HEAD_K ** -0.5)
    k = _l2norm(qkv[..., A_QK_WIDTH:2 * A_QK_WIDTH].reshape(bsz, seq, A_HEADS, A_HEAD_K).astype(f32))
    v = qkv[..., 2 * A_QK_WIDTH:].reshape(bsz, seq, A_HEADS, A_HEAD_V).astype(f32)
    beta = jax.nn.sigmoid(b_raw.astype(f32))
    g = -jnp.exp(A_log.astype(f32)) * jax.nn.softplus(a_raw.astype(f32) + dt_bias.astype(f32))
    o = chunk_gated_delta_rule(q, k, v, beta, g).astype(xn.dtype)
    o = rmsnorm(o, out_norm_w) * jax.nn.silu(z.reshape(bsz, seq, A_HEADS, A_HEAD_V))
    return o.reshape(bsz, seq, A_V_WIDTH) @ w_out


def chunk_attention(xn, w_q, rel_bias, w_out, k_pad, v_pad):
    bsz, seq, _ = xn.shape
    nc = seq // CHUNK
    q = (xn @ w_q).reshape(bsz, nc, CHUNK, B_HEADS, B_HEAD_DIM).transpose(1, 0, 2, 3, 4)
    rel = jnp.arange(CHUNK)[:, None] + LEFT_CHUNKS * CHUNK - jnp.arange(BAND)[None, :]
    bias = rel_bias[:, jnp.clip(rel, -REL_CLIP, REL_CLIP) + REL_CLIP].astype(jnp.float32)
    scale = B_HEAD_DIM ** -0.5

    def one_chunk(args):
        n, q_n = args
        k_band = lax.dynamic_slice_in_dim(k_pad, n * CHUNK, BAND, axis=1)
        v_band = lax.dynamic_slice_in_dim(v_pad, n * CHUNK, BAND, axis=1)
        s = jnp.einsum("bqhd,bkhd->bhqk", q_n, k_band).astype(jnp.float32) * scale + bias
        valid = jnp.arange(BAND) >= (LEFT_CHUNKS - n) * CHUNK
        s = jnp.where(valid, s, NEG_INF)
        p = jax.nn.softmax(s, axis=-1).astype(v_band.dtype)
        return jnp.einsum("bhqk,bkhd->bqhd", p, v_band)

    o = lax.map(one_chunk, (jnp.arange(nc, dtype=jnp.int32), q))
    o = o.transpose(1, 0, 2, 3, 4).reshape(bsz, seq, B_WIDTH)
    return o @ w_out


def conv_ffn(xn, w_up, conv_w, conv_b, w_down):
    h = causal_dwconv(xn @ w_up, conv_w) + conv_b
    gate, val = h[..., :FFN_DIM], h[..., FFN_DIM:]
    return (jax.nn.silu(gate) * val) @ w_down


def setup_inputs(seed: int = 0) -> dict:
    key = jax.random.key(seed)
    ks = jax.random.split(key, 22)

    def nrm(k, shape, scale):
        return jax.random.normal(k, shape, jnp.float32) * scale

    dt = jnp.exp(jax.random.uniform(ks[5], (N_A_LAYERS, A_HEADS), jnp.float32,
                                    minval=math.log(1e-3), maxval=math.log(1e-1)))
    return {
        "x": nrm(ks[0], (BATCH, SEQ, D_MODEL), 1.0),
        "a_norm": 1.0 + nrm(ks[1], (N_A_LAYERS, D_MODEL), 0.02),
        "a_w_in": nrm(ks[2], (N_A_LAYERS, D_MODEL, A_IN_WIDTH), D_MODEL ** -0.5),
        "a_conv": nrm(ks[3], (N_A_LAYERS, A_CONV, A_CONV_WIDTH), A_CONV ** -0.5),
        "a_A_log": jnp.log(jax.random.uniform(ks[4], (N_A_LAYERS, A_HEADS), jnp.float32, minval=1.0, maxval=16.0)),
        "a_dt_bias": dt + jnp.log(-jnp.expm1(-dt)),
        "a_out_norm": 1.0 + nrm(ks[6], (N_A_LAYERS, A_HEAD_V), 0.02),
        "a_w_out": nrm(ks[7], (N_A_LAYERS, A_V_WIDTH, D_MODEL), A_V_WIDTH ** -0.5),
        "kv_norm": 1.0 + nrm(ks[8], (D_MODEL,), 0.02),
        "w_kv": nrm(ks[9], (D_MODEL, 2 * B_WIDTH), D_MODEL ** -0.5),
        "b_norm": 1.0 + nrm(ks[10], (N_B_LAYERS, D_MODEL), 0.02),
        "b_w_q": nrm(ks[11], (N_B_LAYERS, D_MODEL, B_WIDTH), D_MODEL ** -0.5),
        "b_rel_bias": nrm(ks[12], (N_B_LAYERS, B_HEADS, 2 * REL_CLIP + 1), 0.1),
        "b_w_out": nrm(ks[13], (N_B_LAYERS, B_WIDTH, D_MODEL), B_WIDTH ** -0.5),
        "f_norm": 1.0 + nrm(ks[14], (DEPTH, D_MODEL), 0.02),
        "f_w_up": nrm(ks[15], (DEPTH, D_MODEL, 2 * FFN_DIM), D_MODEL ** -0.5),
        "f_conv": nrm(ks[16], (DEPTH, FFN_CONV, 2 * FFN_DIM), FFN_CONV ** -0.5),
        "f_conv_b": nrm(ks[17], (DEPTH, 2 * FFN_DIM), 0.01),
        "f_w_down": nrm(ks[18], (DEPTH, FFN_DIM, D_MODEL), FFN_DIM ** -0.5),
        "final_norm": 1.0 + nrm(ks[19], (D_MODEL,), 0.02),
    }


def reference(x, a_norm, a_w_in, a_conv, a_A_log, a_dt_bias, a_out_norm, a_w_out,
              kv_norm, w_kv, b_norm, b_w_q, b_rel_bias, b_w_out,
              f_norm, f_w_up, f_conv, f_conv_b, f_w_down, final_norm):
    bsz, seq, _ = x.shape
    h = x
    k_pad = None
    v_pad = None
    for layer in range(DEPTH):
        if layer < N_A_LAYERS:
            i = layer
            h = h + gated_deltanet(rmsnorm(h, a_norm[i]), a_w_in[i], a_conv[i], a_A_log[i],
                                   a_dt_bias[i], a_out_norm[i], a_w_out[i])
        else:
            if layer == N_A_LAYERS:
                kv = rmsnorm(h, kv_norm) @ w_kv
                pad = ((0, 0), (LEFT_CHUNKS * CHUNK, 0), (0, 0), (0, 0))
                k_pad = jnp.pad(kv[..., :B_WIDTH].reshape(bsz, seq, B_HEADS, B_HEAD_DIM), pad)
                v_pad = jnp.pad(kv[..., B_WIDTH:].reshape(bsz, seq, B_HEADS, B_HEAD_DIM), pad)
            j = layer - N_A_LAYERS
            h = h + chunk_attention(rmsnorm(h, b_norm[j]), b_w_q[j], b_rel_bias[j], b_w_out[j], k_pad, v_pad)
        h = h + conv_ffn(rmsnorm(h, f_norm[layer]), f_w_up[layer], f_conv[layer], f_conv_b[layer], f_w_down[layer])
    return rmsnorm(h, final_norm)
```

```python
import functools

import jax
import jax.numpy as jnp
from jax import lax
from jax.experimental import pallas as pl
from jax.experimental.pallas import tpu as pltpu

F32 = jnp.float32
BF16 = jnp.bfloat16
EPS = 1e-6
NEG_INF = -1e30

CHUNK = 64
LEFT_CHUNKS = 8
REL_CLIP = 256
A_HEADS = 8
A_HEAD_DIM = 128
B_HEAD_DIM = 64

HALO = 16
FFN_TM = 512
FFN_FC = 256
PROJ_TM = 512
APRE_TM = 256
APRE_NCT = APRE_TM // 64
DEC_ROWS = 8
AREC_CB = 8
ATT_QB = 512
VMEM_LIMIT = 56 * 1024 * 1024


def _rms(x, g):
    ms = jnp.mean(x * x, axis=-1, keepdims=True)
    return x * lax.rsqrt(ms + EPS) * g


def _mm(a, b):
    return jnp.dot(a, b, preferred_element_type=F32)


def _mm_nt(a, b):
    return lax.dot_general(a, b, (((1,), (1,)), ((), ())), preferred_element_type=F32)


def _mm_tn(a, b):
    return lax.dot_general(a, b, (((0,), (0,)), ((), ())), preferred_element_type=F32)


def _const_spec(shape):
    nd = len(shape)
    return pl.BlockSpec(shape, lambda *_: (0,) * nd, pipeline_mode=pl.Buffered(1))


def _params(sem):
    return pltpu.CompilerParams(dimension_semantics=sem, vmem_limit_bytes=VMEM_LIMIT)


def _halo_spec(tm, d):
    hb = tm // HALO
    return pl.BlockSpec((HALO, d), lambda i: (jnp.maximum(i * hb - 1, 0), 0))


def _fill_xn(xn_ref, x, halo, g, i, tiles_per_seq):
    xn_ref[HALO:, :] = _rms(x, g).astype(BF16)
    keep = (i % tiles_per_seq != 0).astype(F32)
    xn_ref[:HALO, :] = (_rms(halo, g) * keep).astype(BF16)


def _ffn_kernel(x_ref, halo_ref, g_ref, wg_ref, wv_ref, cg_ref, cv_ref, wd_ref,
                fg_ref, o_ref, xn_ref, ug_ref, uv_ref, acc_ref, *, tm, tiles_per_seq,
                final_norm):
    x = x_ref[...]
    _fill_xn(xn_ref, x, halo_ref[...], g_ref[...], pl.program_id(0), tiles_per_seq)
    acc_ref[...] = jnp.zeros_like(acc_ref)

    def conv(u_ref, cw):
        return (cw[0:1] * u_ref[pl.ds(HALO - 2, tm), :]
                + cw[1:2] * u_ref[pl.ds(HALO - 1, tm), :]
                + cw[2:3] * u_ref[pl.ds(HALO, tm), :] + cw[3:4])

    def body(c, carry):
        xn = xn_ref[...]
        ug_ref[...] = _mm(xn, wg_ref[c])
        uv_ref[...] = _mm(xn, wv_ref[c])
        gate = conv(ug_ref, cg_ref[c])
        val = conv(uv_ref, cv_ref[c])
        act = gate * jax.nn.sigmoid(gate) * val
        acc_ref[...] += _mm(act.astype(BF16), wd_ref[c])
        return carry

    lax.fori_loop(0, wg_ref.shape[0], body, 0)
    out = x + acc_ref[...]
    if final_norm:
        out = _rms(out, fg_ref[...])
    o_ref[...] = out


def _ffn(h, g, w_up, conv_w, conv_b, w_down, final_g, seq):
    t, d = h.shape
    f = w_down.shape[0]
    tm, fc = FFN_TM, FFN_FC
    nch = f // fc
    wg = w_up[:, :f].reshape(d, nch, fc).transpose(1, 0, 2).astype(BF16)
    wv = w_up[:, f:].reshape(d, nch, fc).transpose(1, 0, 2).astype(BF16)
    cw = jnp.concatenate([conv_w, conv_b[None, :]], axis=0)
    cg = cw[:, :f].reshape(4, nch, fc).transpose(1, 0, 2)
    cv = cw[:, f:].reshape(4, nch, fc).transpose(1, 0, 2)
    wd = w_down.reshape(nch, fc, d).astype(BF16)
    final_norm = final_g is not None
    fg = (final_g if final_norm else g).reshape(1, d)
    kern = functools.partial(_ffn_kernel, tm=tm, tiles_per_seq=seq // tm,
                             final_norm=final_norm)
    return pl.pallas_call(
        kern,
        out_shape=jax.ShapeDtypeStruct((t, d), F32),
        grid=(t // tm,),
        in_specs=[
            pl.BlockSpec((tm, d), lambda i: (i, 0)),
            _halo_spec(tm, d),
            _const_spec((1, d)),
            _const_spec((nch, d, fc)),
            _const_spec((nch, d, fc)),
            _const_spec((nch, 4, fc)),
            _const_spec((nch, 4, fc)),
            _const_spec((nch, fc, d)),
            _const_spec((1, d)),
        ],
        out_specs=pl.BlockSpec((tm, d), lambda i: (i, 0)),
        scratch_shapes=[
            pltpu.VMEM((tm + HALO, d), BF16),
            pltpu.VMEM((tm + HALO, fc), F32),
            pltpu.VMEM((tm + HALO, fc), F32),
            pltpu.VMEM((tm, d), F32),
        ],
        compiler_params=_params(("arbitrary",)),
        name="conv_ffn",
    )(h, h, g.reshape(1, d), wg, wv, cg, cv, wd, fg)


def _norm_proj_kernel(x_ref, g_ref, w_ref, o_ref, *, scale):
    xn = _rms(x_ref[...], g_ref[...]).astype(BF16)
    o_ref[...] = (_mm(xn, w_ref[...]) * scale).astype(o_ref.dtype)


def _norm_proj(h, g, w, scale, name):
    t, d = h.shape
    n = w.shape[1]
    tm = PROJ_TM
    return pl.pallas_call(
        functools.partial(_norm_proj_kernel, scale=scale),
        out_shape=jax.ShapeDtypeStruct((t, n), BF16),
        grid=(t // tm,),
        in_specs=[pl.BlockSpec((tm, d), lambda i: (i, 0)), _const_spec((1, d)),
                  _const_spec((d, n))],
        out_specs=pl.BlockSpec((tm, n), lambda i: (i, 0)),
        compiler_params=_params(("arbitrary",)),
        name=name,
    )(h, g.reshape(1, d), w.astype(BF16))


def _proj_res_kernel(a_ref, w_ref, r_ref, o_ref):
    o_ref[...] = r_ref[...] + _mm(a_ref[...], w_ref[...])


def _proj_res(a, w, res):
    t, k = a.shape
    d = w.shape[1]
    tm = PROJ_TM
    return pl.pallas_call(
        _proj_res_kernel,
        out_shape=jax.ShapeDtypeStruct((t, d), F32),
        grid=(t // tm,),
        in_specs=[pl.BlockSpec((tm, k), lambda i: (i, 0)), _const_spec((k, d)),
                  pl.BlockSpec((tm, d), lambda i: (i, 0))],
        out_specs=pl.BlockSpec((tm, d), lambda i: (i, 0)),
        compiler_params=_params(("arbitrary",)),
        name="proj_res",
    )(a, w.astype(BF16), res)


ATT_PAIR = 2 * CHUNK
ATT_BAND = (LEFT_CHUNKS + 2) * CHUNK


def _attn_kernel(q_ref, kp_ref, kc_ref, vp_ref, vc_ref, b_ref, o_ref, kbuf, vbuf, *, qb):
    i = pl.program_id(2)
    kbuf[0:qb, :] = kp_ref[...]
    kbuf[qb:, :] = kc_ref[...]
    vbuf[0:qb, :] = vp_ref[...]
    vbuf[qb:, :] = vc_ref[...]
    bias = b_ref[0].reshape(2 * ATT_PAIR, ATT_BAND)
    left = lax.broadcasted_iota(jnp.int32, (ATT_PAIR, 128), 1) < B_HEAD_DIM
    col = lax.broadcasted_iota(jnp.int32, (1, ATT_BAND), 1)
    zero = jnp.zeros((), BF16)
    for p in range(qb // ATT_PAIR):
        r0 = p * ATT_PAIR
        qp = q_ref[r0:r0 + ATT_PAIR, :]
        lhs = jnp.concatenate([jnp.where(left, qp, zero), jnp.where(left, zero, qp)], axis=0)
        s = _mm_nt(lhs, kbuf[r0:r0 + ATT_BAND, :])
        first_valid = jnp.where(i == 0, qb - r0, 0)
        s = s + bias + jnp.where(col < first_valid, NEG_INF, 0.0)
        m = jnp.max(s, axis=-1, keepdims=True)
        e = jnp.exp(s - m)
        l = jnp.sum(e, axis=-1, keepdims=True)
        pv = _mm(e.astype(BF16), vbuf[r0:r0 + ATT_BAND, :]) / l
        o = jnp.where(left, pv[:ATT_PAIR], pv[ATT_PAIR:])
        o_ref[r0:r0 + ATT_PAIR, :] = o.astype(o_ref.dtype)


def _band_bias(rel_bias):
    band = (LEFT_CHUNKS + 1) * CHUNK
    rel = jnp.arange(CHUNK)[:, None] + LEFT_CHUNKS * CHUNK - jnp.arange(band)[None, :]
    bias = rel_bias[:, jnp.clip(rel, -REL_CLIP, REL_CLIP) + REL_CLIP].astype(F32)
    top = jnp.pad(bias, ((0, 0), (0, 0), (0, CHUNK)), constant_values=NEG_INF)
    bot = jnp.pad(bias, ((0, 0), (0, 0), (CHUNK, 0)), constant_values=NEG_INF)
    return jnp.concatenate([top, bot], axis=1)


def _band_attn(q, kv, rel_bias, bsz, seq):
    t, width = q.shape
    nhp = width // 128
    qb = ATT_QB
    nq = seq // qb
    bias2 = _band_bias(rel_bias).reshape(nhp, 2, ATT_PAIR, ATT_BAND)

    def cur(b, hp, i):
        return (b * nq + i, hp)

    def prev(b, hp, i):
        return (b * nq + jnp.maximum(i - 1, 0), hp)

    def vcur(b, hp, i):
        return (b * nq + i, nhp + hp)

    def vprev(b, hp, i):
        return (b * nq + jnp.maximum(i - 1, 0), nhp + hp)

    blk = (qb, 128)
    return pl.pallas_call(
        functools.partial(_attn_kernel, qb=qb),
        out_shape=jax.ShapeDtypeStruct((t, width), BF16),
        grid=(bsz, nhp, nq),
        in_specs=[pl.BlockSpec(blk, cur), pl.BlockSpec(blk, prev), pl.BlockSpec(blk, cur),
                  pl.BlockSpec(blk, vprev), pl.BlockSpec(blk, vcur),
                  pl.BlockSpec((1, 2, ATT_PAIR, ATT_BAND), lambda b, hp, i: (hp, 0, 0, 0))],
        out_specs=pl.BlockSpec(blk, cur),
        scratch_shapes=[pltpu.VMEM((2 * qb, 128), BF16), pltpu.VMEM((2 * qb, 128), BF16)],
        compiler_params=_params(("arbitrary", "arbitrary", "arbitrary")),
        name="band_attn",
    )(q, kv, kv, kv, kv, bias2)


def _chunk_prep(q, k, v, beta, g):
    c = CHUNK
    row = lax.broadcasted_iota(jnp.int32, (c, 128), 0)
    lane = lax.broadcasted_iota(jnp.int32, (c, 128), 1)
    col = lane & (c - 1)
    left = lane < c
    gc = g
    for s in (1, 2, 4, 8, 16, 32):
        gc = gc + jnp.where(row >= s, pltpu.roll(gc, s, axis=0), 0.0)
    g_last = gc[c - 1:c, :]
    eg = jnp.exp(gc)
    kb = k * beta
    lhs = jnp.concatenate([kb, q], axis=0).astype(BF16)
    kk = jnp.concatenate([k, k], axis=0).astype(BF16)
    sc = _mm_nt(lhs, kk)
    gcol = jnp.sum(jnp.where(row == col, gc, 0.0), axis=0, keepdims=True)
    causal = row >= col
    decay = jnp.where(causal, jnp.exp(jnp.where(causal, gc - gcol, 0.0)), 0.0)
    m = jnp.where(row > col, sc[:c] * decay, 0.0)
    a = (sc[c:] * decay)[:, :c]
    eye_r = jnp.where(lane == row + c, 1.0, 0.0)
    w_ = jnp.where(left, -m, eye_r)
    for _ in range(6):
        r = _mm(w_[:, :c].astype(BF16), w_.astype(BF16))
        w_ = r + jnp.where(left, 0.0, w_)
    t_inv = w_[:, c:]
    rhs = jnp.concatenate([v * beta, kb * eg], axis=1).astype(BF16)
    uw = _mm(t_inv.astype(BF16), rhs)
    u = uw[:, :128]
    w = uw[:, 128:]
    qd = q * eg
    ke = k * jnp.exp(g_last - gc)
    dec = jnp.exp(g_last)
    return w, qd, ke, u, a, dec


def _a_pre_kernel(x_ref, halo_ref, g_ref, w_ref, cw_ref, hp_ref,
                  wq_ref, ke_ref, u_ref, a_ref, dec_ref, z_ref,
                  xn_ref, pj_ref, *, tm, tiles_per_seq):
    _fill_xn(xn_ref, x_ref[...], halo_ref[...], g_ref[...], pl.program_id(0), tiles_per_seq)
    hd = A_HEAD_DIM
    qk_scale = hd ** -0.5

    def head(h, carry):
        pj_ref[...] = _mm(xn_ref[...], w_ref[h])
        cw = cw_ref[h]
        qkv = (cw[0:1] * pj_ref[pl.ds(HALO - 3, tm), 0:3 * hd]
               + cw[1:2] * pj_ref[pl.ds(HALO - 2, tm), 0:3 * hd]
               + cw[2:3] * pj_ref[pl.ds(HALO - 1, tm), 0:3 * hd]
               + cw[3:4] * pj_ref[pl.ds(HALO, tm), 0:3 * hd])
        qkv = qkv * jax.nn.sigmoid(qkv)
        q = qkv[:, 0:hd]
        k = qkv[:, hd:2 * hd]
        v = qkv[:, 2 * hd:3 * hd]
        q = q * (lax.rsqrt(jnp.sum(q * q, axis=-1, keepdims=True) + EPS) * qk_scale)
        k = k * lax.rsqrt(jnp.sum(k * k, axis=-1, keepdims=True) + EPS)
        z_ref[h] = pj_ref[pl.ds(HALO, tm), 3 * hd:4 * hd]
        beta = jax.nn.sigmoid(pj_ref[pl.ds(HALO, tm), 4 * hd:5 * hd])
        hp = hp_ref[h]
        a_in = pj_ref[pl.ds(HALO, tm), 5 * hd:6 * hd] + hp[1:2]
        softplus = jnp.maximum(a_in, 0.0) + jnp.log1p(jnp.exp(-jnp.abs(a_in)))
        gate = -jnp.exp(hp[0:1]) * softplus
        dec_ref[0, h] = jnp.zeros((DEC_ROWS, hd), F32)
        for c in range(tm // CHUNK):
            r0 = c * CHUNK
            sl = slice(r0, r0 + CHUNK)
            w, qd, ke, u, a, dec = _chunk_prep(q[sl], k[sl], v[sl], beta[sl], gate[sl])
            wq_ref[h, 2 * r0:2 * r0 + CHUNK, :] = w.astype(BF16)
            wq_ref[h, 2 * r0 + CHUNK:2 * r0 + 2 * CHUNK, :] = qd.astype(BF16)
            ke_ref[h, sl, :] = ke.astype(BF16)
            u_ref[h, sl, :] = u
            a_ref[h, sl, :] = a.astype(BF16)
            dec_ref[0, h, c:c + 1, :] = dec
        return carry

    lax.fori_loop(0, w_ref.shape[0], head, 0)


def _a_pre(h, g, w_in, conv_w, a_log, dt_bias, seq):
    t, d = h.shape
    nh, hd = A_HEADS, A_HEAD_DIM
    width = nh * hd
    tm = APRE_TM
    nct = tm // CHUNK
    parts = [w_in[:, s * width:(s + 1) * width].reshape(d, nh, hd) for s in range(4)]
    for s in range(2):
        col = w_in[:, 4 * width + s * nh:4 * width + (s + 1) * nh]
        parts.append(jnp.broadcast_to(col[:, :, None], (d, nh, hd)))
    wcat = jnp.concatenate(parts, axis=2).transpose(1, 0, 2).astype(BF16)
    cw = jnp.concatenate([conv_w[:, s * width:(s + 1) * width].reshape(-1, nh, hd)
                          for s in range(3)], axis=2).transpose(1, 0, 2)
    hp = jnp.broadcast_to(jnp.stack([a_log, dt_bias], axis=1)[:, :, None], (nh, 2, hd))
    kern = functools.partial(_a_pre_kernel, tm=tm, tiles_per_seq=seq // tm)
    out_shape = (
        jax.ShapeDtypeStruct((nh, 2 * t, hd), BF16),
        jax.ShapeDtypeStruct((nh, t, hd), BF16),
        jax.ShapeDtypeStruct((nh, t, hd), F32),
        jax.ShapeDtypeStruct((nh, t, CHUNK), BF16),
        jax.ShapeDtypeStruct((t // tm, nh, DEC_ROWS, hd), F32),
        jax.ShapeDtypeStruct((nh, t, hd), F32),
    )
    return pl.pallas_call(
        kern,
        out_shape=out_shape,
        grid=(t // tm,),
        in_specs=[pl.BlockSpec((tm, d), lambda i: (i, 0)), _halo_spec(tm, d),
                  _const_spec((1, d)), _const_spec((nh, d, 6 * hd)),
                  _const_spec((nh, 4, 3 * hd)), _const_spec((nh, 2, hd))],
        out_specs=(
            pl.BlockSpec((nh, 2 * tm, hd), lambda i: (0, i, 0)),
            pl.BlockSpec((nh, tm, hd), lambda i: (0, i, 0)),
            pl.BlockSpec((nh, tm, hd), lambda i: (0, i, 0)),
            pl.BlockSpec((nh, tm, CHUNK), lambda i: (0, i, 0)),
            pl.BlockSpec((1, nh, DEC_ROWS, hd), lambda i: (i, 0, 0, 0)),
            pl.BlockSpec((nh, tm, hd), lambda i: (0, i, 0)),
        ),
        scratch_shapes=[pltpu.VMEM((tm + HALO, d), BF16), pltpu.VMEM((tm + HALO, 6 * hd), F32)],
        compiler_params=_params(("arbitrary",)),
        name="a_pre",
    )(h, h, g.reshape(1, d), wcat, cw, hp)


def _a_rec_kernel(wq_ref, ke_ref, u_ref, a_ref, dec_ref, z_ref, h_ref, onw_ref, wo_ref,
                  out_ref, state_ref, o_ref, y_ref, *, cb):
    nh, hd = A_HEADS, A_HEAD_DIM

    @pl.when(pl.program_id(1) == 0)
    def _():
        state_ref[...] = jnp.zeros_like(state_ref)

    def chunk(c, carry):
        r0 = pl.multiple_of(c * CHUNK, CHUNK)
        r1 = pl.multiple_of(c * 2 * CHUNK, 2 * CHUNK)
        for h in range(nh):
            st = state_ref[h]
            ws = _mm(wq_ref[h, pl.ds(r1, 2 * CHUNK), :], st.astype(BF16))
            v_new = (u_ref[h, pl.ds(r0, CHUNK), :] - ws[:CHUNK]).astype(BF16)
            o_ref[h, pl.ds(r0, CHUNK), :] = ws[CHUNK:] + _mm(a_ref[h, pl.ds(r0, CHUNK), :], v_new)
            dec = dec_ref[c // APRE_NCT, h, pl.ds(c % APRE_NCT, 1), :]
            state_ref[h] = st * dec + _mm_tn(ke_ref[h, pl.ds(r0, CHUNK), :], v_new)
        return carry

    lax.fori_loop(0, cb, chunk, 0)
    onw = onw_ref[...]
    for h in range(nh):
        o = o_ref[h]
        z = z_ref[h]
        y_ref[:, h * hd:(h + 1) * hd] = (_rms(o, onw) * (z * jax.nn.sigmoid(z))).astype(BF16)
    out_ref[...] = h_ref[...] + _mm(y_ref[...], wo_ref[...])


def _a_rec(pre, h, out_norm_w, w_out, bsz, seq):
    wq, ke, u, a, dec, z = pre
    t, d = h.shape
    nh, hd = A_HEADS, A_HEAD_DIM
    cb = AREC_CB
    rows = cb * CHUNK
    nblk = seq // rows

    def idx(b, j):
        return (0, b * nblk + j, 0)

    return pl.pallas_call(
        functools.partial(_a_rec_kernel, cb=cb),
        out_shape=jax.ShapeDtypeStruct((t, d), F32),
        grid=(bsz, nblk),
        in_specs=[pl.BlockSpec((nh, 2 * rows, hd), idx), pl.BlockSpec((nh, rows, hd), idx),
                  pl.BlockSpec((nh, rows, hd), idx), pl.BlockSpec((nh, rows, CHUNK), idx),
                  pl.BlockSpec((cb // APRE_NCT, nh, DEC_ROWS, hd),
                               lambda b, j: (b * nblk + j, 0, 0, 0)),
                  pl.BlockSpec((nh, rows, hd), idx),
                  pl.BlockSpec((rows, d), lambda b, j: (b * nblk + j, 0)),
                  _const_spec((1, hd)), _const_spec((nh * hd, d))],
        out_specs=pl.BlockSpec((rows, d), lambda b, j: (b * nblk + j, 0)),
        scratch_shapes=[pltpu.VMEM((nh, hd, hd), F32), pltpu.VMEM((nh, rows, hd), F32),
                        pltpu.VMEM((rows, nh * hd), BF16)],
        compiler_params=_params(("arbitrary", "arbitrary")),
        name="a_rec",
    )(wq, ke, u, a, dec, z, h, out_norm_w.reshape(1, hd), w_out.astype(BF16))


def kernel(x, a_norm, a_w_in, a_conv, a_A_log, a_dt_bias, a_out_norm, a_w_out, kv_norm, w_kv, b_norm, b_w_q, b_rel_bias, b_w_out, f_norm, f_w_up, f_conv, f_conv_b, f_w_down, final_norm):
    bsz, seq, d = x.shape
    n_a = a_norm.shape[0]
    n_b = b_norm.shape[0]
    depth = n_a + n_b
    h = x.reshape(bsz * seq, d)
    kv = None
    for layer in range(depth):
        if layer < n_a:
            i = layer
            pre = _a_pre(h, a_norm[i], a_w_in[i], a_conv[i], a_A_log[i], a_dt_bias[i], seq)
            h = _a_rec(pre, h, a_out_norm[i], a_w_out[i], bsz, seq)
        else:
            j = layer - n_a
            if kv is None:
                kv = _norm_proj(h, kv_norm, w_kv, 1.0, "kv_proj")
            q = _norm_proj(h, b_norm[j], b_w_q[j], B_HEAD_DIM ** -0.5, "q_proj")
            o = _band_attn(q, kv, b_rel_bias[j], bsz, seq)
            h = _proj_res(o, b_w_out[j], h)
        fg = final_norm if layer == depth - 1 else None
        h = _ffn(h, f_norm[layer], f_w_up[layer], f_conv[layer], f_conv_b[layer],
                 f_w_down[layer], fg, seq)
    return h.reshape(bsz, seq, d)
```

```python
import functools

import jax
import jax.numpy as jnp
from jax import lax
from jax.experimental import pallas as pl
from jax.experimental.pallas import tpu as pltpu

F32 = jnp.float32
BF16 = jnp.bfloat16
EPS = 1e-6
NEG_INF = -1e30

CHUNK = 64
LEFT_CHUNKS = 8
REL_CLIP = 256
A_HEADS = 8
A_HEAD_DIM = 128
B_HEAD_DIM = 64

HALO = 16
FFN_TM = 512
FFN_FC = 256
PROJ_TM = 512
APRE_TM = 512
APRE_NCT = APRE_TM // 64
DEC_ROWS = 8
assert APRE_NCT == DEC_ROWS
AREC_CB = 8
ATT_QB = 512
VMEM_LIMIT = 56 * 1024 * 1024


def _rms(x, g):
    ms = jnp.mean(x * x, axis=-1, keepdims=True)
    return x * lax.rsqrt(ms + EPS) * g


def _mm(a, b):
    return jnp.dot(a, b, preferred_element_type=F32)


def _mm_nt(a, b):
    return lax.dot_general(a, b, (((1,), (1,)), ((), ())), preferred_element_type=F32)


def _mm_tn(a, b):
    return lax.dot_general(a, b, (((0,), (0,)), ((), ())), preferred_element_type=F32)


def _const_spec(shape):
    nd = len(shape)
    return pl.BlockSpec(shape, lambda *_: (0,) * nd, pipeline_mode=pl.Buffered(1))


def _params(sem):
    return pltpu.CompilerParams(dimension_semantics=sem, vmem_limit_bytes=VMEM_LIMIT)


def _halo_spec(tm, d):
    hb = tm // HALO
    return pl.BlockSpec((HALO, d), lambda i: (jnp.maximum(i * hb - 1, 0), 0))


def _fill_xn(xn_ref, x, halo, g, i, tiles_per_seq):
    xn_ref[HALO:, :] = _rms(x, g).astype(BF16)
    keep = (i % tiles_per_seq != 0).astype(F32)
    xn_ref[:HALO, :] = (_rms(halo, g) * keep).astype(BF16)


def _ffn_kernel(x_ref, halo_ref, g_ref, wu_ref, cw_ref, wd_ref, fg_ref, o_ref,
                xn_ref, u_ref, act_ref, *, tm, fc, tiles_per_seq, final_norm):
    x = x_ref[...]
    _fill_xn(xn_ref, x, halo_ref[...], g_ref[...], pl.program_id(0), tiles_per_seq)
    f = wd_ref.shape[0]
    nch = f // fc

    def up(c):
        xn = xn_ref[...]
        for half in range(2):
            lo = half * f + c * fc
            u_ref[c % 2, half] = _mm(xn, wu_ref[:, lo:lo + fc])

    def conv(c, half):
        lo = half * f + c * fc
        cw = cw_ref[:, lo:lo + fc]
        u = u_ref.at[c % 2, half]
        return (cw[0:1] * u[pl.ds(HALO - 2, tm), :] + cw[1:2] * u[pl.ds(HALO - 1, tm), :]
                + cw[2:3] * u[pl.ds(HALO, tm), :] + cw[3:4])

    up(0)
    for c in range(nch):
        if c + 1 < nch:
            up(c + 1)
        gate = conv(c, 0)
        act_ref[:, c * fc:(c + 1) * fc] = (gate * jax.nn.sigmoid(gate) * conv(c, 1)).astype(BF16)
    out = x + _mm(act_ref[...], wd_ref[...])
    if final_norm:
        out = _rms(out, fg_ref[...])
    o_ref[...] = out


def _ffn(h, g, w_up, conv_w, conv_b, w_down, final_g, seq):
    t, d = h.shape
    f = w_down.shape[0]
    tm, fc = FFN_TM, FFN_FC
    cw = jnp.concatenate([conv_w, conv_b[None, :]], axis=0)
    final_norm = final_g is not None
    fg = (final_g if final_norm else g).reshape(1, d)
    kern = functools.partial(_ffn_kernel, tm=tm, fc=fc, tiles_per_seq=seq // tm,
                             final_norm=final_norm)
    return pl.pallas_call(
        kern,
        out_shape=jax.ShapeDtypeStruct((t, d), F32),
        grid=(t // tm,),
        in_specs=[
            pl.BlockSpec((tm, d), lambda i: (i, 0)),
            _halo_spec(tm, d),
            _const_spec((1, d)),
            _const_spec((d, 2 * f)),
            _const_spec((4, 2 * f)),
            _const_spec((f, d)),
            _const_spec((1, d)),
        ],
        out_specs=pl.BlockSpec((tm, d), lambda i: (i, 0)),
        scratch_shapes=[
            pltpu.VMEM((tm + HALO, d), BF16),
            pltpu.VMEM((2, 2, tm + HALO, fc), F32),
            pltpu.VMEM((tm, f), BF16),
        ],
        compiler_params=_params(("arbitrary",)),
        name="conv_ffn",
    )(h, h, g.reshape(1, d), w_up.astype(BF16), cw, w_down.astype(BF16), fg)


def _norm_proj_kernel(x_ref, g_ref, w_ref, o_ref, *, scale):
    xn = _rms(x_ref[...], g_ref[...]).astype(BF16)
    o_ref[...] = (_mm(xn, w_ref[...]) * scale).astype(o_ref.dtype)


def _norm_proj(h, g, w, scale, name):
    t, d = h.shape
    n = w.shape[1]
    tm = PROJ_TM
    return pl.pallas_call(
        functools.partial(_norm_proj_kernel, scale=scale),
        out_shape=jax.ShapeDtypeStruct((t, n), BF16),
        grid=(t // tm,),
        in_specs=[pl.BlockSpec((tm, d), lambda i: (i, 0)), _const_spec((1, d)),
                  _const_spec((d, n))],
        out_specs=pl.BlockSpec((tm, n), lambda i: (i, 0)),
        compiler_params=_params(("arbitrary",)),
        name=name,
    )(h, g.reshape(1, d), w.astype(BF16))


def _proj_res_kernel(a_ref, w_ref, r_ref, o_ref):
    o_ref[...] = r_ref[...] + _mm(a_ref[...], w_ref[...])


def _proj_res(a, w, res):
    t, k = a.shape
    d = w.shape[1]
    tm = PROJ_TM
    return pl.pallas_call(
        _proj_res_kernel,
        out_shape=jax.ShapeDtypeStruct((t, d), F32),
        grid=(t // tm,),
        in_specs=[pl.BlockSpec((tm, k), lambda i: (i, 0)), _const_spec((k, d)),
                  pl.BlockSpec((tm, d), lambda i: (i, 0))],
        out_specs=pl.BlockSpec((tm, d), lambda i: (i, 0)),
        compiler_params=_params(("arbitrary",)),
        name="proj_res",
    )(a, w.astype(BF16), res)


ATT_PAIR = 2 * CHUNK
ATT_BAND = (LEFT_CHUNKS + 2) * CHUNK


def _attn_kernel(q_ref, kp_ref, kc_ref, vp_ref, vc_ref, b_ref, o_ref, kbuf, vbuf, *, qb):
    i = pl.program_id(2)
    kbuf[0:qb, :] = kp_ref[...]
    kbuf[qb:, :] = kc_ref[...]
    vbuf[0:qb, :] = vp_ref[...]
    vbuf[qb:, :] = vc_ref[...]
    bias = b_ref[0].reshape(2 * ATT_PAIR, ATT_BAND)
    left = lax.broadcasted_iota(jnp.int32, (ATT_PAIR, 128), 1) < B_HEAD_DIM
    col = lax.broadcasted_iota(jnp.int32, (1, ATT_BAND), 1)
    zero = jnp.zeros((), BF16)

    def scores(p):
        r0 = p * ATT_PAIR
        qp = q_ref[r0:r0 + ATT_PAIR, :]
        lhs = jnp.concatenate([jnp.where(left, qp, zero), jnp.where(left, zero, qp)], axis=0)
        return _mm_nt(lhs, kbuf[r0:r0 + ATT_BAND, :])

    npair = qb // ATT_PAIR
    s_next = scores(0)
    for p in range(npair):
        r0 = p * ATT_PAIR
        s = s_next
        if p + 1 < npair:
            s_next = scores(p + 1)
        first_valid = jnp.where(i == 0, qb - r0, 0)
        s = s + bias + jnp.where(col < first_valid, NEG_INF, 0.0)
        m = jnp.max(s, axis=-1, keepdims=True)
        e = jnp.exp(s - m)
        l = jnp.sum(e, axis=-1, keepdims=True)
        pv = _mm(e.astype(BF16), vbuf[r0:r0 + ATT_BAND, :]) / l
        o = jnp.where(left, pv[:ATT_PAIR], pv[ATT_PAIR:])
        o_ref[r0:r0 + ATT_PAIR, :] = o.astype(o_ref.dtype)


def _band_bias(rel_bias):
    band = (LEFT_CHUNKS + 1) * CHUNK
    rel = jnp.arange(CHUNK)[:, None] + LEFT_CHUNKS * CHUNK - jnp.arange(band)[None, :]
    bias = rel_bias[:, jnp.clip(rel, -REL_CLIP, REL_CLIP) + REL_CLIP].astype(F32)
    top = jnp.pad(bias, ((0, 0), (0, 0), (0, CHUNK)), constant_values=NEG_INF)
    bot = jnp.pad(bias, ((0, 0), (0, 0), (CHUNK, 0)), constant_values=NEG_INF)
    return jnp.concatenate([top, bot], axis=1)


def _band_attn(q, kv, rel_bias, bsz, seq):
    t, width = q.shape
    nhp = width // 128
    qb = ATT_QB
    nq = seq // qb
    bias2 = _band_bias(rel_bias).reshape(nhp, 2, ATT_PAIR, ATT_BAND)

    def cur(b, hp, i):
        return (b * nq + i, hp)

    def prev(b, hp, i):
        return (b * nq + jnp.maximum(i - 1, 0), hp)

    def vcur(b, hp, i):
        return (b * nq + i, nhp + hp)

    def vprev(b, hp, i):
        return (b * nq + jnp.maximum(i - 1, 0), nhp + hp)

    blk = (qb, 128)
    return pl.pallas_call(
        functools.partial(_attn_kernel, qb=qb),
        out_shape=jax.ShapeDtypeStruct((t, width), BF16),
        grid=(bsz, nhp, nq),
        in_specs=[pl.BlockSpec(blk, cur), pl.BlockSpec(blk, prev), pl.BlockSpec(blk, cur),
                  pl.BlockSpec(blk, vprev), pl.BlockSpec(blk, vcur),
                  pl.BlockSpec((1, 2, ATT_PAIR, ATT_BAND), lambda b, hp, i: (hp, 0, 0, 0))],
        out_specs=pl.BlockSpec(blk, cur),
        scratch_shapes=[pltpu.VMEM((2 * qb, 128), BF16), pltpu.VMEM((2 * qb, 128), BF16)],
        compiler_params=_params(("arbitrary", "arbitrary", "arbitrary")),
        name="band_attn",
    )(q, kv, kv, kv, kv, bias2)


def _wy_prep(q, k, v, beta, g):
    c = CHUNK
    rows = q.shape[0]
    n = rows // c
    rin = lax.broadcasted_iota(jnp.int32, (rows, 128), 0) & (c - 1)
    gc = g
    for s in (1, 2, 4, 8, 16, 32):
        gc = gc + jnp.where(rin >= s, pltpu.roll(gc, s, axis=0), 0.0)
    eg = jnp.exp(gc)
    kb = k * beta
    qd = q * eg
    rhs = jnp.concatenate([v * beta, kb * eg], axis=1).astype(BF16)
    kb16 = kb.astype(BF16)
    q16 = q.astype(BF16)
    k16 = k.astype(BF16)
    row = lax.broadcasted_iota(jnp.int32, (c, 128), 0)
    lane = lax.broadcasted_iota(jnp.int32, (c, 128), 1)
    col = lane & (c - 1)
    left = lane < c
    causal = row >= col
    strict = row > col
    eye_r = jnp.where(lane == row + c, 1.0, 0.0)
    sl = [slice(i * c, (i + 1) * c) for i in range(n)]
    g_last = [gc[(i + 1) * c - 1:(i + 1) * c, :] for i in range(n)]
    sc = [_mm_nt(jnp.concatenate([kb16[s], q16[s]], axis=0),
                 jnp.concatenate([k16[s], k16[s]], axis=0)) for s in sl]
    ws, a_out = [], []
    for i, s in enumerate(sl):
        gci = gc[s]
        gcol = jnp.sum(jnp.where(row == col, gci, 0.0), axis=0, keepdims=True)
        decay = jnp.where(causal, jnp.exp(jnp.where(causal, gci - gcol, 0.0)), 0.0)
        m = jnp.where(strict, sc[i][:c] * decay, 0.0)
        a_out.append((sc[i][c:] * decay)[:, :c])
        ws.append(jnp.where(left, -m, eye_r))
    for _ in range(6):
        rs = [_mm(w[:, :c].astype(BF16), w.astype(BF16)) for w in ws]
        ws = [r + jnp.where(left, 0.0, w) for r, w in zip(rs, ws)]
    uw = [_mm(w[:, c:].astype(BF16), rhs[s]) for w, s in zip(ws, sl)]
    ke = jnp.concatenate([k[s] * jnp.exp(gl - gc[s]) for s, gl in zip(sl, g_last)], axis=0)
    dec = jnp.concatenate([jnp.exp(gl) for gl in g_last], axis=0)
    return uw, a_out, qd, ke, dec


def _a_pre_kernel(x_ref, halo_ref, g_ref, w_ref, cw_ref, hp_ref,
                  wq_ref, ke_ref, u_ref, a_ref, dec_ref, z_ref,
                  xn_ref, pj_ref, *, tm, tiles_per_seq):
    _fill_xn(xn_ref, x_ref[...], halo_ref[...], g_ref[...], pl.program_id(0), tiles_per_seq)
    hd = A_HEAD_DIM
    qk_scale = hd ** -0.5

    def head(h, carry):
        pj_ref[...] = _mm(xn_ref[...], w_ref[h])
        cw = cw_ref[h]
        qkv = (cw[0:1] * pj_ref[pl.ds(HALO - 3, tm), 0:3 * hd]
               + cw[1:2] * pj_ref[pl.ds(HALO - 2, tm), 0:3 * hd]
               + cw[2:3] * pj_ref[pl.ds(HALO - 1, tm), 0:3 * hd]
               + cw[3:4] * pj_ref[pl.ds(HALO, tm), 0:3 * hd])
        qkv = qkv * jax.nn.sigmoid(qkv)
        q = qkv[:, 0:hd]
        k = qkv[:, hd:2 * hd]
        v = qkv[:, 2 * hd:3 * hd]
        q = q * (lax.rsqrt(jnp.sum(q * q, axis=-1, keepdims=True) + EPS) * qk_scale)
        k = k * lax.rsqrt(jnp.sum(k * k, axis=-1, keepdims=True) + EPS)
        z_ref[h] = pj_ref[pl.ds(HALO, tm), 3 * hd:4 * hd]
        beta = jax.nn.sigmoid(pj_ref[pl.ds(HALO, tm), 4 * hd:5 * hd])
        hp = hp_ref[h]
        a_in = pj_ref[pl.ds(HALO, tm), 5 * hd:6 * hd] + hp[1:2]
        softplus = jnp.maximum(a_in, 0.0) + jnp.log1p(jnp.exp(-jnp.abs(a_in)))
        gate = -jnp.exp(hp[0:1]) * softplus
        uw, a_out, qd, ke, dec = _wy_prep(q, k, v, beta, gate)
        for c in range(tm // CHUNK):
            r0 = c * CHUNK
            wq_ref[h, 2 * r0:2 * r0 + CHUNK, :] = uw[c][:, hd:].astype(BF16)
            wq_ref[h, 2 * r0 + CHUNK:2 * r0 + 2 * CHUNK, :] = qd[r0:r0 + CHUNK].astype(BF16)
            u_ref[h, r0:r0 + CHUNK, :] = uw[c][:, :hd]
            a_ref[h, r0:r0 + CHUNK, :] = a_out[c].astype(BF16)
        ke_ref[h] = ke.astype(BF16)
        dec_ref[0, h] = dec
        return carry

    lax.fori_loop(0, w_ref.shape[0], head, 0)


def _a_pre(h, g, w_in, conv_w, a_log, dt_bias, seq):
    t, d = h.shape
    nh, hd = A_HEADS, A_HEAD_DIM
    width = nh * hd
    tm = APRE_TM
    nct = tm // CHUNK
    parts = [w_in[:, s * width:(s + 1) * width].reshape(d, nh, hd) for s in range(4)]
    for s in range(2):
        col = w_in[:, 4 * width + s * nh:4 * width + (s + 1) * nh]
        parts.append(jnp.broadcast_to(col[:, :, None], (d, nh, hd)))
    wcat = jnp.concatenate(parts, axis=2).transpose(1, 0, 2).astype(BF16)
    cw = jnp.concatenate([conv_w[:, s * width:(s + 1) * width].reshape(-1, nh, hd)
                          for s in range(3)], axis=2).transpose(1, 0, 2)
    hp = jnp.broadcast_to(jnp.stack([a_log, dt_bias], axis=1)[:, :, None], (nh, 2, hd))
    kern = functools.partial(_a_pre_kernel, tm=tm, tiles_per_seq=seq // tm)
    out_shape = (
        jax.ShapeDtypeStruct((nh, 2 * t, hd), BF16),
        jax.ShapeDtypeStruct((nh, t, hd), BF16),
        jax.ShapeDtypeStruct((nh, t, hd), F32),
        jax.ShapeDtypeStruct((nh, t, CHUNK), BF16),
        jax.ShapeDtypeStruct((t // tm, nh, DEC_ROWS, hd), F32),
        jax.ShapeDtypeStruct((nh, t, hd), F32),
    )
    return pl.pallas_call(
        kern,
        out_shape=out_shape,
        grid=(t // tm,),
        in_specs=[pl.BlockSpec((tm, d), lambda i: (i, 0)), _halo_spec(tm, d),
                  _const_spec((1, d)), _const_spec((nh, d, 6 * hd)),
                  _const_spec((nh, 4, 3 * hd)), _const_spec((nh, 2, hd))],
        out_specs=(
            pl.BlockSpec((nh, 2 * tm, hd), lambda i: (0, i, 0)),
            pl.BlockSpec((nh, tm, hd), lambda i: (0, i, 0)),
            pl.BlockSpec((nh, tm, hd), lambda i: (0, i, 0)),
            pl.BlockSpec((nh, tm, CHUNK), lambda i: (0, i, 0)),
            pl.BlockSpec((1, nh, DEC_ROWS, hd), lambda i: (i, 0, 0, 0)),
            pl.BlockSpec((nh, tm, hd), lambda i: (0, i, 0)),
        ),
        scratch_shapes=[pltpu.VMEM((tm + HALO, d), BF16), pltpu.VMEM((tm + HALO, 6 * hd), F32)],
        compiler_params=_params(("arbitrary",)),
        name="a_pre",
    )(h, h, g.reshape(1, d), wcat, cw, hp)


def _a_rec_kernel(wq_ref, ke_ref, u_ref, a_ref, dec_ref, z_ref, h_ref, onw_ref, wo_ref,
                  out_ref, state_ref, o_ref, y_ref, *, cb):
    nh, hd = A_HEADS, A_HEAD_DIM

    @pl.when(pl.program_id(1) == 0)
    def _():
        state_ref[...] = jnp.zeros_like(state_ref)

    def chunk(c, carry):
        r0 = pl.multiple_of(c * CHUNK, CHUNK)
        r1 = pl.multiple_of(c * 2 * CHUNK, 2 * CHUNK)
        heads = range(nh)
        st = [state_ref[h] for h in heads]
        ws = [_mm(wq_ref[h, pl.ds(r1, 2 * CHUNK), :], st[h].astype(BF16)) for h in heads]
        v_new = [(u_ref[h, pl.ds(r0, CHUNK), :] - ws[h][:CHUNK]).astype(BF16) for h in heads]
        av = [_mm(a_ref[h, pl.ds(r0, CHUNK), :], v_new[h]) for h in heads]
        upd = [_mm_tn(ke_ref[h, pl.ds(r0, CHUNK), :], v_new[h]) for h in heads]
        for h in heads:
            o_ref[h, pl.ds(r0, CHUNK), :] = ws[h][CHUNK:] + av[h]
            dec = dec_ref[c // APRE_NCT, h, pl.ds(c % APRE_NCT, 1), :]
            state_ref[h] = st[h] * dec + upd[h]
        return carry

    lax.fori_loop(0, cb, chunk, 0)
    onw = onw_ref[...]
    for h in range(nh):
        o = o_ref[h]
        z = z_ref[h]
        y_ref[:, h * hd:(h + 1) * hd] = (_rms(o, onw) * (z * jax.nn.sigmoid(z))).astype(BF16)
    out_ref[...] = h_ref[...] + _mm(y_ref[...], wo_ref[...])


def _a_rec(pre, h, out_norm_w, w_out, bsz, seq):
    wq, ke, u, a, dec, z = pre
    t, d = h.shape
    nh, hd = A_HEADS, A_HEAD_DIM
    cb = AREC_CB
    rows = cb * CHUNK
    nblk = seq // rows

    def idx(b, j):
        return (0, b * nblk + j, 0)

    return pl.pallas_call(
        functools.partial(_a_rec_kernel, cb=cb),
        out_shape=jax.ShapeDtypeStruct((t, d), F32),
        grid=(bsz, nblk),
        in_specs=[pl.BlockSpec((nh, 2 * rows, hd), idx), pl.BlockSpec((nh, rows, hd), idx),
                  pl.BlockSpec((nh, rows, hd), idx), pl.BlockSpec((nh, rows, CHUNK), idx),
                  pl.BlockSpec((cb // APRE_NCT, nh, DEC_ROWS, hd),
                               lambda b, j: (b * nblk + j, 0, 0, 0)),
                  pl.BlockSpec((nh, rows, hd), idx),
                  pl.BlockSpec((rows, d), lambda b, j: (b * nblk + j, 0)),
                  _const_spec((1, hd)), _const_spec((nh * hd, d))],
        out_specs=pl.BlockSpec((rows, d), lambda b, j: (b * nblk + j, 0)),
        scratch_shapes=[pltpu.VMEM((nh, hd, hd), F32), pltpu.VMEM((nh, rows, hd), F32),
                        pltpu.VMEM((rows, nh * hd), BF16)],
        compiler_params=_params(("arbitrary", "arbitrary")),
        name="a_rec",
    )(wq, ke, u, a, dec, z, h, out_norm_w.reshape(1, hd), w_out.astype(BF16))


def kernel(x, a_norm, a_w_in, a_conv, a_A_log, a_dt_bias, a_out_norm, a_w_out, kv_norm, w_kv, b_norm, b_w_q, b_rel_bias, b_w_out, f_norm, f_w_up, f_conv, f_conv_b, f_w_down, final_norm):
    bsz, seq, d = x.shape
    n_a = a_norm.shape[0]
    n_b = b_norm.shape[0]
    depth = n_a + n_b
    h = x.reshape(bsz * seq, d)
    kv = None
    for layer in range(depth):
        if layer < n_a:
            i = layer
            pre = _a_pre(h, a_norm[i], a_w_in[i], a_conv[i], a_A_log[i], a_dt_bias[i], seq)
            h = _a_rec(pre, h, a_out_norm[i], a_w_out[i], bsz, seq)
        else:
            j = layer - n_a
            if kv is None:
                kv = _norm_proj(h, kv_norm, w_kv, 1.0, "kv_proj")
            q = _norm_proj(h, b_norm[j], b_w_q[j], B_HEAD_DIM ** -0.5, "q_proj")
            o = _band_attn(q, kv, b_rel_bias[j], bsz, seq)
            h = _proj_res(o, b_w_out[j], h)
        fg = final_norm if layer == depth - 1 else None
        h = _ffn(h, f_norm[layer], f_w_up[layer], f_conv[layer], f_conv_b[layer],
                 f_w_down[layer], fg, seq)
    return h.reshape(bsz, seq, d)
```

```python
import functools

import jax
import jax.numpy as jnp
from jax import lax
from jax.experimental import pallas as pl
from jax.experimental.pallas import tpu as pltpu

F32 = jnp.float32
BF16 = jnp.bfloat16
EPS = 1e-6
NEG_INF = -1e30
LOG2E = 1.4426950408889634

CHUNK = 64
LEFT_CHUNKS = 8
REL_CLIP = 256
A_HEADS = 8
A_HEAD_DIM = 128
B_HEAD_DIM = 64

HALO = 16
FFN_TM = 512
FFN_FC = 256
PROJ_TM = 512
APRE_TM = 512
APRE_NCT = APRE_TM // 64
DEC_ROWS = 8
assert APRE_NCT == DEC_ROWS
AREC_CB = 8
ATT_QB = 1024
VMEM_LIMIT = 56 * 1024 * 1024


def _rms(x, g):
    ms = jnp.mean(x * x, axis=-1, keepdims=True)
    return x * lax.rsqrt(ms + EPS) * g


def _mm(a, b):
    return jnp.dot(a, b, preferred_element_type=F32)


def _mm_nt(a, b):
    return lax.dot_general(a, b, (((1,), (1,)), ((), ())), preferred_element_type=F32)


def _mm_tn(a, b):
    return lax.dot_general(a, b, (((0,), (0,)), ((), ())), preferred_element_type=F32)


def _const_spec(shape):
    nd = len(shape)
    return pl.BlockSpec(shape, lambda *_: (0,) * nd, pipeline_mode=pl.Buffered(1))


def _params(sem):
    return pltpu.CompilerParams(dimension_semantics=sem, vmem_limit_bytes=VMEM_LIMIT)


def _halo_spec(tm, d):
    hb = tm // HALO
    return pl.BlockSpec((HALO, d), lambda i: (jnp.maximum(i * hb - 1, 0), 0))


def _fill_xn(xn_ref, x, halo, g, i, tiles_per_seq):
    xn_ref[HALO:, :] = _rms(x, g).astype(BF16)
    keep = (i % tiles_per_seq != 0).astype(F32)
    xn_ref[:HALO, :] = (_rms(halo, g) * keep).astype(BF16)


def _ffn_kernel(x_ref, halo_ref, g_ref, wu_ref, cw_ref, wd_ref, fg_ref, o_ref,
                xn_ref, u_ref, act_ref, *, tm, fc, tiles_per_seq, final_norm):
    x = x_ref[...]
    _fill_xn(xn_ref, x, halo_ref[...], g_ref[...], pl.program_id(0), tiles_per_seq)
    f = wd_ref.shape[0]
    nch = f // fc

    def up(c):
        xn = xn_ref[...]
        for half in range(2):
            lo = half * f + c * fc
            u_ref[c % 2, half] = _mm(xn, wu_ref[:, lo:lo + fc])

    def conv(c, half):
        lo = half * f + c * fc
        cw = cw_ref[:, lo:lo + fc]
        u = u_ref.at[c % 2, half]
        return (cw[0:1] * u[pl.ds(HALO - 2, tm), :] + cw[1:2] * u[pl.ds(HALO - 1, tm), :]
                + cw[2:3] * u[pl.ds(HALO, tm), :] + cw[3:4])

    up(0)
    for c in range(nch):
        if c + 1 < nch:
            up(c + 1)
        gate = conv(c, 0)
        act_ref[:, c * fc:(c + 1) * fc] = (gate * jax.nn.sigmoid(gate) * conv(c, 1)).astype(BF16)
    out = x + _mm(act_ref[...], wd_ref[...])
    if final_norm:
        out = _rms(out, fg_ref[...])
    o_ref[...] = out


def _ffn(h, g, w_up, conv_w, conv_b, w_down, final_g, seq):
    t, d = h.shape
    f = w_down.shape[0]
    tm, fc = FFN_TM, FFN_FC
    cw = jnp.concatenate([conv_w, conv_b[None, :]], axis=0)
    final_norm = final_g is not None
    fg = (final_g if final_norm else g).reshape(1, d)
    kern = functools.partial(_ffn_kernel, tm=tm, fc=fc, tiles_per_seq=seq // tm,
                             final_norm=final_norm)
    return pl.pallas_call(
        kern,
        out_shape=jax.ShapeDtypeStruct((t, d), F32),
        grid=(t // tm,),
        in_specs=[
            pl.BlockSpec((tm, d), lambda i: (i, 0)),
            _halo_spec(tm, d),
            _const_spec((1, d)),
            _const_spec((d, 2 * f)),
            _const_spec((4, 2 * f)),
            _const_spec((f, d)),
            _const_spec((1, d)),
        ],
        out_specs=pl.BlockSpec((tm, d), lambda i: (i, 0)),
        scratch_shapes=[
            pltpu.VMEM((tm + HALO, d), BF16),
            pltpu.VMEM((2, 2, tm + HALO, fc), F32),
            pltpu.VMEM((tm, f), BF16),
        ],
        compiler_params=_params(("arbitrary",)),
        name="conv_ffn",
    )(h, h, g.reshape(1, d), w_up.astype(BF16), cw, w_down.astype(BF16), fg)


def _norm_proj_kernel(x_ref, g_ref, w_ref, o_ref, *, scale):
    xn = _rms(x_ref[...], g_ref[...]).astype(BF16)
    o_ref[...] = (_mm(xn, w_ref[...]) * scale).astype(o_ref.dtype)


def _norm_proj(h, g, w, scale, name):
    t, d = h.shape
    n = w.shape[1]
    tm = PROJ_TM
    return pl.pallas_call(
        functools.partial(_norm_proj_kernel, scale=scale),
        out_shape=jax.ShapeDtypeStruct((t, n), BF16),
        grid=(t // tm,),
        in_specs=[pl.BlockSpec((tm, d), lambda i: (i, 0)), _const_spec((1, d)),
                  _const_spec((d, n))],
        out_specs=pl.BlockSpec((tm, n), lambda i: (i, 0)),
        compiler_params=_params(("arbitrary",)),
        name=name,
    )(h, g.reshape(1, d), w.astype(BF16))


def _kv_proj_kernel(x_ref, g_ref, wk_ref, wvt_ref, k_ref, vt_ref):
    xn = _rms(x_ref[...], g_ref[...]).astype(BF16)
    k_ref[...] = _mm(xn, wk_ref[...]).astype(k_ref.dtype)
    vt_ref[...] = _mm_nt(wvt_ref[...], xn).astype(vt_ref.dtype)


def _kv_proj(h, g, w_kv):
    t, d = h.shape
    width = w_kv.shape[1] // 2
    tm = PROJ_TM
    return pl.pallas_call(
        _kv_proj_kernel,
        out_shape=(jax.ShapeDtypeStruct((t, width), BF16), jax.ShapeDtypeStruct((width, t), BF16)),
        grid=(t // tm,),
        in_specs=[pl.BlockSpec((tm, d), lambda i: (i, 0)), _const_spec((1, d)),
                  _const_spec((d, width)), _const_spec((width, d))],
        out_specs=(pl.BlockSpec((tm, width), lambda i: (i, 0)),
                   pl.BlockSpec((width, tm), lambda i: (0, i))),
        compiler_params=_params(("arbitrary",)),
        name="kv_proj",
    )(h, g.reshape(1, d), w_kv[:, :width].astype(BF16), w_kv[:, width:].T.astype(BF16))


def _proj_res_kernel(a_ref, w_ref, r_ref, o_ref):
    o_ref[...] = r_ref[...] + _mm(a_ref[...], w_ref[...])


def _proj_res(a, w, res):
    t, k = a.shape
    d = w.shape[1]
    tm = PROJ_TM
    return pl.pallas_call(
        _proj_res_kernel,
        out_shape=jax.ShapeDtypeStruct((t, d), F32),
        grid=(t // tm,),
        in_specs=[pl.BlockSpec((tm, k), lambda i: (i, 0)), _const_spec((k, d)),
                  pl.BlockSpec((tm, d), lambda i: (i, 0))],
        out_specs=pl.BlockSpec((tm, d), lambda i: (i, 0)),
        compiler_params=_params(("arbitrary",)),
        name="proj_res",
    )(a, w.astype(BF16), res)


ATT_PAIR = 2 * CHUNK
ATT_BAND = (LEFT_CHUNKS + 2) * CHUNK
ATT_HIST = LEFT_CHUNKS * CHUNK
ATT_SLOTS = 4


def _attn_kernel(q_ref, kp_ref, kc_ref, vp_ref, vc_ref, b_ref, o_ref,
                 kbuf, vbuf, bhist, s_ref, e_ref, *, qb):
    i = pl.program_id(2)
    kbuf[0:ATT_HIST, :] = kp_ref[...]
    kbuf[ATT_HIST:, :] = kc_ref[...]
    vbuf[:, 0:ATT_HIST] = vp_ref[...]
    vbuf[:, ATT_HIST:] = vc_ref[...]
    left = lax.broadcasted_iota(jnp.int32, (ATT_PAIR, 128), 1) < B_HEAD_DIM
    zero = jnp.zeros((), BF16)
    nhist = ATT_HIST // ATT_PAIR
    hd = B_HEAD_DIM

    @pl.when(i == 0)
    def _():
        key = lax.broadcasted_iota(jnp.int32, (ATT_BAND, 1), 0)
        for p in range(nhist):
            bhist[p] = b_ref[0] + jnp.where(key < ATT_HIST - p * ATT_PAIR, NEG_INF, 0.0)

    @pl.when(i == 1)
    def _():
        for p in range(nhist):
            bhist[p] = b_ref[0]

    def scores(p):
        r0 = p * ATT_PAIR
        qp = q_ref[r0:r0 + ATT_PAIR, :]
        qq = jnp.concatenate([jnp.where(left, qp, zero), jnp.where(left, zero, qp)], axis=0)
        s_ref[p % ATT_SLOTS] = _mm_nt(kbuf[r0:r0 + ATT_BAND, :], qq)

    npair = qb // ATT_PAIR
    ahead = ATT_SLOTS - 1
    for p in range(min(ahead, npair)):
        scores(p)
    for p in range(npair):
        r0 = p * ATT_PAIR
        if p + ahead < npair:
            scores(p + ahead)
        b = bhist[p] if p < nhist else b_ref[0]
        m = jnp.max(s_ref[p % ATT_SLOTS] + b, axis=0, keepdims=True)
        e = jnp.exp2((s_ref[p % ATT_SLOTS] - m) + b)
        l = jnp.sum(e, axis=0, keepdims=True)
        e_ref[p % ATT_SLOTS] = e.astype(BF16)
        ot = _mm(vbuf[:, r0:r0 + ATT_BAND], e_ref[p % ATT_SLOTS]) / l
        ot = jnp.concatenate([ot[0:hd, 0:ATT_PAIR], ot[hd:2 * hd, ATT_PAIR:]], axis=0)
        o_ref[r0:r0 + ATT_PAIR, :] = ot.T.astype(o_ref.dtype)


def _band_bias(rel_bias):
    band = (LEFT_CHUNKS + 1) * CHUNK
    rel = jnp.arange(CHUNK)[:, None] + LEFT_CHUNKS * CHUNK - jnp.arange(band)[None, :]
    bias = rel_bias[:, jnp.clip(rel, -REL_CLIP, REL_CLIP) + REL_CLIP].astype(F32)
    bias = bias * LOG2E
    top = jnp.pad(bias, ((0, 0), (0, 0), (0, CHUNK)), constant_values=NEG_INF)
    bot = jnp.pad(bias, ((0, 0), (0, 0), (CHUNK, 0)), constant_values=NEG_INF)
    return jnp.concatenate([top, bot], axis=1)


def _band_attn(q, k, vt, rel_bias, bsz, seq):
    t, width = q.shape
    nhp = width // 128
    qb = ATT_QB
    nq = seq // qb
    assert nq >= 2 and qb % ATT_HIST == 0
    hpb = qb // ATT_HIST
    bias_t = _band_bias(rel_bias).reshape(nhp, 2 * ATT_PAIR, ATT_BAND).transpose(0, 2, 1)

    def cur(b, hp, i):
        return (b * nq + i, hp)

    def prev(b, hp, i):
        return (jnp.maximum((b * nq + i) * hpb - 1, 0), hp)

    def vcur(b, hp, i):
        return (hp, b * nq + i)

    def vprev(b, hp, i):
        return (hp, jnp.maximum((b * nq + i) * hpb - 1, 0))

    return pl.pallas_call(
        functools.partial(_attn_kernel, qb=qb),
        out_shape=jax.ShapeDtypeStruct((t, width), BF16),
        grid=(bsz, nhp, nq),
        in_specs=[pl.BlockSpec((qb, 128), cur),
                  pl.BlockSpec((ATT_HIST, 128), prev), pl.BlockSpec((qb, 128), cur),
                  pl.BlockSpec((128, ATT_HIST), vprev), pl.BlockSpec((128, qb), vcur),
                  pl.BlockSpec((1, ATT_BAND, 2 * ATT_PAIR), lambda b, hp, i: (hp, 0, 0))],
        out_specs=pl.BlockSpec((qb, 128), cur),
        scratch_shapes=[pltpu.VMEM((ATT_HIST + qb, 128), BF16), pltpu.VMEM((128, ATT_HIST + qb), BF16),
                        pltpu.VMEM((ATT_HIST // ATT_PAIR, ATT_BAND, 2 * ATT_PAIR), F32),
                        pltpu.VMEM((ATT_SLOTS, ATT_BAND, 2 * ATT_PAIR), F32),
                        pltpu.VMEM((ATT_SLOTS, ATT_BAND, 2 * ATT_PAIR), BF16)],
        compiler_params=_params(("arbitrary", "arbitrary", "arbitrary")),
        name="band_attn",
    )(q, k, k, vt, vt, bias_t)


def _lane_col(x, idx):
    lane = lax.broadcasted_iota(jnp.int32, x.shape, 1)
    return jnp.sum(jnp.where(lane == idx, x, 0.0), axis=-1, keepdims=True)


def _wy_prep(q, k, v, beta, gc, eg, kef):
    c = CHUNK
    rows = q.shape[0]
    n = rows // c
    kb = k * beta
    qd = q * eg
    ke = k * kef
    rhs = jnp.concatenate([v * beta, kb * eg], axis=1).astype(BF16)
    kb16 = kb.astype(BF16)
    q16 = q.astype(BF16)
    k16 = k.astype(BF16)
    row = lax.broadcasted_iota(jnp.int32, (c, 128), 0)
    lane = lax.broadcasted_iota(jnp.int32, (c, 128), 1)
    col = lane & (c - 1)
    left = lane < c
    causal = row >= col
    strict = row > col
    eye_r = jnp.where(lane == row + c, 1.0, 0.0)
    sl = [slice(i * c, (i + 1) * c) for i in range(n)]
    sc = [_mm_nt(jnp.concatenate([kb16[s], q16[s]], axis=0),
                 jnp.concatenate([k16[s], k16[s]], axis=0)) for s in sl]
    ws, a_out = [], []
    for i, s in enumerate(sl):
        gci = jnp.broadcast_to(gc[s], (c, 128))
        gcol = jnp.sum(jnp.where(row == col, gci, 0.0), axis=0, keepdims=True)
        decay = jnp.where(causal, jnp.exp(jnp.where(causal, gci - gcol, 0.0)), 0.0)
        m = jnp.where(strict, sc[i][:c] * decay, 0.0)
        a_out.append((sc[i][c:] * decay)[:, :c])
        ws.append(jnp.where(left, -m, eye_r))
    for _ in range(6):
        rs = [_mm(w[:, :c].astype(BF16), w.astype(BF16)) for w in ws]
        ws = [r + jnp.where(left, 0.0, w) for r, w in zip(rs, ws)]
    uw = [_mm(w[:, c:].astype(BF16), rhs[s]) for w, s in zip(ws, sl)]
    return uw, a_out, qd, ke


def _a_pre_kernel(x_ref, halo_ref, g_ref, w_ref, wg_ref, cw_ref, hp_ref,
                  wq_ref, ke_ref, u_ref, a_ref, dec_ref, z_ref,
                  xn_ref, pj_ref, *, tm, tiles_per_seq):
    _fill_xn(xn_ref, x_ref[...], halo_ref[...], g_ref[...], pl.program_id(0), tiles_per_seq)
    nh, hd, c = A_HEADS, A_HEAD_DIM, CHUNK
    n = tm // c
    qk_scale = hd ** -0.5

    gr = _mm(xn_ref[HALO:, :], wg_ref[...])
    hp = hp_ref[...]
    beta_all = jax.nn.sigmoid(gr)
    a_in = gr + hp[1:2]
    g_all = -jnp.exp(hp[0:1]) * (jnp.maximum(a_in, 0.0) + jnp.log1p(jnp.exp(-jnp.abs(a_in))))
    rin = lax.broadcasted_iota(jnp.int32, (tm, 128), 0) & (c - 1)
    gc_all = g_all
    for s in (1, 2, 4, 8, 16, 32):
        gc_all = gc_all + jnp.where(rin >= s, pltpu.roll(gc_all, s, axis=0), 0.0)
    g_last = [gc_all[(i + 1) * c - 1:(i + 1) * c, :] for i in range(n)]
    eg_all = jnp.exp(gc_all)
    kef_all = jnp.exp(jnp.concatenate([jnp.broadcast_to(gl, (c, 128)) for gl in g_last], axis=0) - gc_all)
    dec_all = jnp.exp(jnp.concatenate(g_last, axis=0))

    def project(h):
        pj_ref[h % 2] = _mm(xn_ref[...], w_ref[h])

    project(0)
    for h in range(nh):
        if h + 1 < nh:
            project(h + 1)
        pj = pj_ref.at[h % 2]
        cw = cw_ref[h]
        qkv = (cw[0:1] * pj[pl.ds(HALO - 3, tm), 0:3 * hd]
               + cw[1:2] * pj[pl.ds(HALO - 2, tm), 0:3 * hd]
               + cw[2:3] * pj[pl.ds(HALO - 1, tm), 0:3 * hd]
               + cw[3:4] * pj[pl.ds(HALO, tm), 0:3 * hd])
        qkv = qkv * jax.nn.sigmoid(qkv)
        q = qkv[:, 0:hd]
        k = qkv[:, hd:2 * hd]
        v = qkv[:, 2 * hd:3 * hd]
        q = q * (lax.rsqrt(jnp.sum(q * q, axis=-1, keepdims=True) + EPS) * qk_scale)
        k = k * lax.rsqrt(jnp.sum(k * k, axis=-1, keepdims=True) + EPS)
        z_ref[h] = pj[pl.ds(HALO, tm), 3 * hd:4 * hd]
        uw, a_out, qd, ke = _wy_prep(q, k, v, _lane_col(beta_all, h), _lane_col(gc_all, nh + h),
                                     _lane_col(eg_all, nh + h), _lane_col(kef_all, nh + h))
        for i in range(n):
            r0 = i * c
            wq_ref[h, 2 * r0:2 * r0 + c, :] = uw[i][:, hd:].astype(BF16)
            wq_ref[h, 2 * r0 + c:2 * r0 + 2 * c, :] = qd[r0:r0 + c].astype(BF16)
            u_ref[h, r0:r0 + c, :] = uw[i][:, :hd]
            a_ref[h, r0:r0 + c, :] = a_out[i].astype(BF16)
        ke_ref[h] = ke.astype(BF16)
        dec_ref[0, h] = jnp.broadcast_to(_lane_col(dec_all, nh + h), (n, hd))


def _a_pre(h, g, w_in, conv_w, a_log, dt_bias, seq):
    t, d = h.shape
    nh, hd = A_HEADS, A_HEAD_DIM
    width = nh * hd
    tm = APRE_TM
    wcat = jnp.concatenate([w_in[:, s * width:(s + 1) * width].reshape(d, nh, hd) for s in range(4)],
                           axis=2).transpose(1, 0, 2).astype(BF16)
    wgate = jnp.pad(w_in[:, 4 * width:], ((0, 0), (0, 128 - 2 * nh))).astype(BF16)
    cw = jnp.concatenate([conv_w[:, s * width:(s + 1) * width].reshape(-1, nh, hd)
                          for s in range(3)], axis=2).transpose(1, 0, 2)
    hp = jnp.pad(jnp.stack([a_log, dt_bias], axis=0), ((0, 0), (nh, 128 - 2 * nh)))
    kern = functools.partial(_a_pre_kernel, tm=tm, tiles_per_seq=seq // tm)
    out_shape = (
        jax.ShapeDtypeStruct((nh, 2 * t, hd), BF16),
        jax.ShapeDtypeStruct((nh, t, hd), BF16),
        jax.ShapeDtypeStruct((nh, t, hd), F32),
        jax.ShapeDtypeStruct((nh, t, CHUNK), BF16),
        jax.ShapeDtypeStruct((t // tm, nh, DEC_ROWS, hd), F32),
        jax.ShapeDtypeStruct((nh, t, hd), F32),
    )
    return pl.pallas_call(
        kern,
        out_shape=out_shape,
        grid=(t // tm,),
        in_specs=[pl.BlockSpec((tm, d), lambda i: (i, 0)), _halo_spec(tm, d),
                  _const_spec((1, d)), _const_spec((nh, d, 4 * hd)), _const_spec((d, 128)),
                  _const_spec((nh, 4, 3 * hd)), _const_spec((2, 128))],
        out_specs=(
            pl.BlockSpec((nh, 2 * tm, hd), lambda i: (0, i, 0)),
            pl.BlockSpec((nh, tm, hd), lambda i: (0, i, 0)),
            pl.BlockSpec((nh, tm, hd), lambda i: (0, i, 0)),
            pl.BlockSpec((nh, tm, CHUNK), lambda i: (0, i, 0)),
            pl.BlockSpec((1, nh, DEC_ROWS, hd), lambda i: (i, 0, 0, 0)),
            pl.BlockSpec((nh, tm, hd), lambda i: (0, i, 0)),
        ),
        scratch_shapes=[pltpu.VMEM((tm + HALO, d), BF16), pltpu.VMEM((2, tm + HALO, 4 * hd), F32)],
        compiler_params=_params(("arbitrary",)),
        name="a_pre",
    )(h, h, g.reshape(1, d), wcat, wgate, cw, hp)


def _a_rec_kernel(wq_ref, ke_ref, u_ref, a_ref, dec_ref, z_ref, h_ref, onw_ref, wo_ref,
                  out_ref, state_ref, o_ref, y_ref, *, cb):
    nh, hd = A_HEADS, A_HEAD_DIM

    @pl.when(pl.program_id(1) == 0)
    def _():
        state_ref[...] = jnp.zeros_like(state_ref)

    def chunk(c, carry):
        r0 = pl.multiple_of(c * CHUNK, CHUNK)
        r1 = pl.multiple_of(c * 2 * CHUNK, 2 * CHUNK)
        heads = range(nh)
        st = [state_ref[h] for h in heads]
        ws = [_mm(wq_ref[h, pl.ds(r1, 2 * CHUNK), :], st[h].astype(BF16)) for h in heads]
        v_new = [(u_ref[h, pl.ds(r0, CHUNK), :] - ws[h][:CHUNK]).astype(BF16) for h in heads]
        av = [_mm(a_ref[h, pl.ds(r0, CHUNK), :], v_new[h]) for h in heads]
        upd = [_mm_tn(ke_ref[h, pl.ds(r0, CHUNK), :], v_new[h]) for h in heads]
        for h in heads:
            o_ref[h, pl.ds(r0, CHUNK), :] = ws[h][CHUNK:] + av[h]
            dec = dec_ref[c // APRE_NCT, h, pl.ds(c % APRE_NCT, 1), :]
            state_ref[h] = st[h] * dec + upd[h]
        return carry

    lax.fori_loop(0, cb, chunk, 0)
    onw = onw_ref[...]
    for h in range(nh):
        o = o_ref[h]
        z = z_ref[h]
        y_ref[:, h * hd:(h + 1) * hd] = (_rms(o, onw) * (z * jax.nn.sigmoid(z))).astype(BF16)
    out_ref[...] = h_ref[...] + _mm(y_ref[...], wo_ref[...])


def _a_rec(pre, h, out_norm_w, w_out, bsz, seq):
    wq, ke, u, a, dec, z = pre
    t, d = h.shape
    nh, hd = A_HEADS, A_HEAD_DIM
    cb = AREC_CB
    rows = cb * CHUNK
    nblk = seq // rows

    def idx(b, j):
        return (0, b * nblk + j, 0)

    return pl.pallas_call(
        functools.partial(_a_rec_kernel, cb=cb),
        out_shape=jax.ShapeDtypeStruct((t, d), F32),
        grid=(bsz, nblk),
        in_specs=[pl.BlockSpec((nh, 2 * rows, hd), idx), pl.BlockSpec((nh, rows, hd), idx),
                  pl.BlockSpec((nh, rows, hd), idx), pl.BlockSpec((nh, rows, CHUNK), idx),
                  pl.BlockSpec((cb // APRE_NCT, nh, DEC_ROWS, hd),
                               lambda b, j: (b * nblk + j, 0, 0, 0)),
                  pl.BlockSpec((nh, rows, hd), idx),
                  pl.BlockSpec((rows, d), lambda b, j: (b * nblk + j, 0)),
                  _const_spec((1, hd)), _const_spec((nh * hd, d))],
        out_specs=pl.BlockSpec((rows, d), lambda b, j: (b * nblk + j, 0)),
        scratch_shapes=[pltpu.VMEM((nh, hd, hd), F32), pltpu.VMEM((nh, rows, hd), F32),
                        pltpu.VMEM((rows, nh * hd), BF16)],
        compiler_params=_params(("arbitrary", "arbitrary")),
        name="a_rec",
    )(wq, ke, u, a, dec, z, h, out_norm_w.reshape(1, hd), w_out.astype(BF16))


def kernel(x, a_norm, a_w_in, a_conv, a_A_log, a_dt_bias, a_out_norm, a_w_out, kv_norm, w_kv, b_norm, b_w_q, b_rel_bias, b_w_out, f_norm, f_w_up, f_conv, f_conv_b, f_w_down, final_norm):
    bsz, seq, d = x.shape
    n_a = a_norm.shape[0]
    n_b = b_norm.shape[0]
    depth = n_a + n_b
    h = x.reshape(bsz * seq, d)
    kv = None
    for layer in range(depth):
        if layer < n_a:
            i = layer
            pre = _a_pre(h, a_norm[i], a_w_in[i], a_conv[i], a_A_log[i], a_dt_bias[i], seq)
            h = _a_rec(pre, h, a_out_norm[i], a_w_out[i], bsz, seq)
        else:
            j = layer - n_a
            if kv is None:
                kv = _kv_proj(h, kv_norm, w_kv)
            q = _norm_proj(h, b_norm[j], b_w_q[j], B_HEAD_DIM ** -0.5 * LOG2E, "q_proj")
            o = _band_attn(q, kv[0], kv[1], b_rel_bias[j], bsz, seq)
            h = _proj_res(o, b_w_out[j], h)
        fg = final_norm if layer == depth - 1 else None
        h = _ffn(h, f_norm[layer], f_w_up[layer], f_conv[layer], f_conv_b[layer],
                 f_w_down[layer], fg, seq)
    return h.reshape(bsz, seq, d)
```

```python
import functools

import jax
import jax.numpy as jnp
from jax import lax
from jax.experimental import pallas as pl
from jax.experimental.pallas import tpu as pltpu

F32 = jnp.float32
BF16 = jnp.bfloat16
EPS = 1e-6
NEG_INF = -1e30
LOG2E = 1.4426950408889634

CHUNK = 64
LEFT_CHUNKS = 8
REL_CLIP = 256
A_HEADS = 8
A_HEAD_DIM = 128
B_HEAD_DIM = 64

HALO = 16
FFN_TM = 512
FFN_FC = 256
PROJ_TM = 512
APRE_TM = 512
APRE_NCT = APRE_TM // 64
DEC_ROWS = 8
assert APRE_NCT == DEC_ROWS
AREC_CB = 2
AREC_NB = 4
ATT_QB = 1024
VMEM_LIMIT = 56 * 1024 * 1024


def _rms(x, g):
    ms = jnp.mean(x * x, axis=-1, keepdims=True)
    return x * lax.rsqrt(ms + EPS) * g


def _mm(a, b):
    return jnp.dot(a, b, preferred_element_type=F32)


def _mm_nt(a, b):
    return lax.dot_general(a, b, (((1,), (1,)), ((), ())), preferred_element_type=F32)


def _mm_tn(a, b):
    return lax.dot_general(a, b, (((0,), (0,)), ((), ())), preferred_element_type=F32)


def _const_spec(shape):
    nd = len(shape)
    return pl.BlockSpec(shape, lambda *_: (0,) * nd, pipeline_mode=pl.Buffered(1))


def _params(sem):
    return pltpu.CompilerParams(dimension_semantics=sem, vmem_limit_bytes=VMEM_LIMIT)


def _halo_spec(tm, d):
    hb = tm // HALO
    return pl.BlockSpec((HALO, d), lambda i: (jnp.maximum(i * hb - 1, 0), 0))


def _fill_xn(xn_ref, x, halo, g, i, tiles_per_seq):
    xn_ref[HALO:, :] = _rms(x, g).astype(BF16)
    keep = (i % tiles_per_seq != 0).astype(F32)
    xn_ref[:HALO, :] = (_rms(halo, g) * keep).astype(BF16)


def _ffn_kernel(x_ref, halo_ref, g_ref, wu_ref, cw_ref, wd_ref, fg_ref, o_ref,
                xn_ref, u_ref, act_ref, *, tm, fc, tiles_per_seq, final_norm):
    x = x_ref[...]
    _fill_xn(xn_ref, x, halo_ref[...], g_ref[...], pl.program_id(0), tiles_per_seq)
    f = wd_ref.shape[0]
    nch = f // fc

    def up(c):
        xn = xn_ref[...]
        for half in range(2):
            lo = half * f + c * fc
            u_ref[c % 2, half] = _mm(xn, wu_ref[:, lo:lo + fc])

    def conv(c, half):
        lo = half * f + c * fc
        cw = cw_ref[:, lo:lo + fc]
        u = u_ref.at[c % 2, half]
        return (cw[0:1] * u[pl.ds(HALO - 2, tm), :] + cw[1:2] * u[pl.ds(HALO - 1, tm), :]
                + cw[2:3] * u[pl.ds(HALO, tm), :] + cw[3:4])

    up(0)
    for c in range(nch):
        if c + 1 < nch:
            up(c + 1)
        gate = conv(c, 0)
        act_ref[:, c * fc:(c + 1) * fc] = (gate * jax.nn.sigmoid(gate) * conv(c, 1)).astype(BF16)
    out = x + _mm(act_ref[...], wd_ref[...])
    if final_norm:
        out = _rms(out, fg_ref[...])
    o_ref[...] = out


def _ffn(h, g, w_up, conv_w, conv_b, w_down, final_g, seq):
    t, d = h.shape
    f = w_down.shape[0]
    tm, fc = FFN_TM, FFN_FC
    cw = jnp.concatenate([conv_w, conv_b[None, :]], axis=0)
    final_norm = final_g is not None
    fg = (final_g if final_norm else g).reshape(1, d)
    kern = functools.partial(_ffn_kernel, tm=tm, fc=fc, tiles_per_seq=seq // tm,
                             final_norm=final_norm)
    return pl.pallas_call(
        kern,
        out_shape=jax.ShapeDtypeStruct((t, d), F32),
        grid=(t // tm,),
        in_specs=[
            pl.BlockSpec((tm, d), lambda i: (i, 0)),
            _halo_spec(tm, d),
            _const_spec((1, d)),
            _const_spec((d, 2 * f)),
            _const_spec((4, 2 * f)),
            _const_spec((f, d)),
            _const_spec((1, d)),
        ],
        out_specs=pl.BlockSpec((tm, d), lambda i: (i, 0)),
        scratch_shapes=[
            pltpu.VMEM((tm + HALO, d), BF16),
            pltpu.VMEM((2, 2, tm + HALO, fc), F32),
            pltpu.VMEM((tm, f), BF16),
        ],
        compiler_params=_params(("arbitrary",)),
        name="conv_ffn",
    )(h, h, g.reshape(1, d), w_up.astype(BF16), cw, w_down.astype(BF16), fg)


def _norm_proj_kernel(x_ref, g_ref, w_ref, o_ref, *, scale):
    xn = _rms(x_ref[...], g_ref[...]).astype(BF16)
    o_ref[...] = (_mm(xn, w_ref[...]) * scale).astype(o_ref.dtype)


def _norm_proj(h, g, w, scale, name):
    t, d = h.shape
    n = w.shape[1]
    tm = PROJ_TM
    return pl.pallas_call(
        functools.partial(_norm_proj_kernel, scale=scale),
        out_shape=jax.ShapeDtypeStruct((t, n), BF16),
        grid=(t // tm,),
        in_specs=[pl.BlockSpec((tm, d), lambda i: (i, 0)), _const_spec((1, d)),
                  _const_spec((d, n))],
        out_specs=pl.BlockSpec((tm, n), lambda i: (i, 0)),
        compiler_params=_params(("arbitrary",)),
        name=name,
    )(h, g.reshape(1, d), w.astype(BF16))


def _kv_proj_kernel(x_ref, g_ref, wk_ref, wvt_ref, k_ref, vt_ref):
    xn = _rms(x_ref[...], g_ref[...]).astype(BF16)
    k_ref[...] = _mm(xn, wk_ref[...]).astype(k_ref.dtype)
    vt_ref[...] = _mm_nt(wvt_ref[...], xn).astype(vt_ref.dtype)


def _kv_proj(h, g, w_kv):
    t, d = h.shape
    width = w_kv.shape[1] // 2
    tm = PROJ_TM
    return pl.pallas_call(
        _kv_proj_kernel,
        out_shape=(jax.ShapeDtypeStruct((t, width), BF16), jax.ShapeDtypeStruct((width, t), BF16)),
        grid=(t // tm,),
        in_specs=[pl.BlockSpec((tm, d), lambda i: (i, 0)), _const_spec((1, d)),
                  _const_spec((d, width)), _const_spec((width, d))],
        out_specs=(pl.BlockSpec((tm, width), lambda i: (i, 0)),
                   pl.BlockSpec((width, tm), lambda i: (0, i))),
        compiler_params=_params(("arbitrary",)),
        name="kv_proj",
    )(h, g.reshape(1, d), w_kv[:, :width].astype(BF16), w_kv[:, width:].T.astype(BF16))


def _proj_res_kernel(a_ref, w_ref, r_ref, o_ref):
    o_ref[...] = r_ref[...] + _mm(a_ref[...], w_ref[...])


def _proj_res(a, w, res):
    t, k = a.shape
    d = w.shape[1]
    tm = PROJ_TM
    return pl.pallas_call(
        _proj_res_kernel,
        out_shape=jax.ShapeDtypeStruct((t, d), F32),
        grid=(t // tm,),
        in_specs=[pl.BlockSpec((tm, k), lambda i: (i, 0)), _const_spec((k, d)),
                  pl.BlockSpec((tm, d), lambda i: (i, 0))],
        out_specs=pl.BlockSpec((tm, d), lambda i: (i, 0)),
        compiler_params=_params(("arbitrary",)),
        name="proj_res",
    )(a, w.astype(BF16), res)


ATT_PAIR = 2 * CHUNK
ATT_BAND = (LEFT_CHUNKS + 2) * CHUNK
ATT_HIST = LEFT_CHUNKS * CHUNK
ATT_SLOTS = 8


def _attn_kernel(q_ref, kp_ref, kc_ref, vp_ref, vc_ref, b_ref, o_ref,
                 kbuf, vbuf, bhist, s_ref, e_ref, *, qb):
    i = pl.program_id(2)
    kbuf[0:ATT_HIST, :] = kp_ref[...]
    kbuf[ATT_HIST:, :] = kc_ref[...]
    vbuf[:, 0:ATT_HIST] = vp_ref[...]
    vbuf[:, ATT_HIST:] = vc_ref[...]
    left = lax.broadcasted_iota(jnp.int32, (ATT_PAIR, 128), 1) < B_HEAD_DIM
    zero = jnp.zeros((), BF16)
    nhist = ATT_HIST // ATT_PAIR
    hd = B_HEAD_DIM

    @pl.when(i == 0)
    def _():
        key = lax.broadcasted_iota(jnp.int32, (ATT_BAND, 1), 0)
        for p in range(nhist):
            bhist[p] = b_ref[0] + jnp.where(key < ATT_HIST - p * ATT_PAIR, NEG_INF, 0.0)

    @pl.when(i == 1)
    def _():
        for p in range(nhist):
            bhist[p] = b_ref[0]

    def scores(p):
        r0 = p * ATT_PAIR
        qp = q_ref[r0:r0 + ATT_PAIR, :]
        qq = jnp.concatenate([jnp.where(left, qp, zero), jnp.where(left, zero, qp)], axis=0)
        s_ref[p % ATT_SLOTS] = _mm_nt(kbuf[r0:r0 + ATT_BAND, :], qq)

    npair = qb // ATT_PAIR
    ahead = ATT_SLOTS - 1
    for p in range(min(ahead, npair)):
        scores(p)
    for p in range(npair):
        r0 = p * ATT_PAIR
        if p + ahead < npair:
            scores(p + ahead)
        b = bhist[p] if p < nhist else b_ref[0]
        m = jnp.max(s_ref[p % ATT_SLOTS] + b, axis=0, keepdims=True)
        e = jnp.exp2((s_ref[p % ATT_SLOTS] - m) + b)
        l = jnp.sum(e, axis=0, keepdims=True)
        e_ref[p % ATT_SLOTS] = e.astype(BF16)
        ot = _mm(vbuf[:, r0:r0 + ATT_BAND], e_ref[p % ATT_SLOTS]) / l
        ot = jnp.concatenate([ot[0:hd, 0:ATT_PAIR], ot[hd:2 * hd, ATT_PAIR:]], axis=0)
        o_ref[r0:r0 + ATT_PAIR, :] = ot.T.astype(o_ref.dtype)


def _band_bias(rel_bias):
    c = CHUNK
    band = (LEFT_CHUNKS + 1) * c
    nh = rel_bias.shape[0]
    assert c - 1 <= REL_CLIP < band - 1
    ramp = rel_bias[:, REL_CLIP - (c - 1):]
    g = jnp.concatenate([ramp, jnp.broadcast_to(rel_bias[:, -1:], (nh, band - 1 - REL_CLIP))], axis=1)
    period = band + c
    u = jnp.pad(g, ((0, 0), (0, period - g.shape[1])))
    hk = jnp.tile(u, (1, c + 1))[:, :c * (period + 1)].reshape(nh, c, period + 1)[:, :, :band]
    bias = hk[:, :, ::-1].astype(F32) * LOG2E
    top = jnp.pad(bias, ((0, 0), (0, 0), (0, CHUNK)), constant_values=NEG_INF)
    bot = jnp.pad(bias, ((0, 0), (0, 0), (CHUNK, 0)), constant_values=NEG_INF)
    return jnp.concatenate([top, bot], axis=1)


def _band_attn(q, k, vt, rel_bias, bsz, seq):
    t, width = q.shape
    nhp = width // 128
    qb = ATT_QB
    nq = seq // qb
    assert nq >= 2 and qb % ATT_HIST == 0
    hpb = qb // ATT_HIST
    bias_t = _band_bias(rel_bias).reshape(nhp, 2 * ATT_PAIR, ATT_BAND).transpose(0, 2, 1)

    def cur(b, hp, i):
        return (b * nq + i, hp)

    def prev(b, hp, i):
        return (jnp.maximum((b * nq + i) * hpb - 1, 0), hp)

    def vcur(b, hp, i):
        return (hp, b * nq + i)

    def vprev(b, hp, i):
        return (hp, jnp.maximum((b * nq + i) * hpb - 1, 0))

    return pl.pallas_call(
        functools.partial(_attn_kernel, qb=qb),
        out_shape=jax.ShapeDtypeStruct((t, width), BF16),
        grid=(bsz, nhp, nq),
        in_specs=[pl.BlockSpec((qb, 128), cur),
                  pl.BlockSpec((ATT_HIST, 128), prev), pl.BlockSpec((qb, 128), cur),
                  pl.BlockSpec((128, ATT_HIST), vprev), pl.BlockSpec((128, qb), vcur),
                  pl.BlockSpec((1, ATT_BAND, 2 * ATT_PAIR), lambda b, hp, i: (hp, 0, 0))],
        out_specs=pl.BlockSpec((qb, 128), cur),
        scratch_shapes=[pltpu.VMEM((ATT_HIST + qb, 128), BF16), pltpu.VMEM((128, ATT_HIST + qb), BF16),
                        pltpu.VMEM((ATT_HIST // ATT_PAIR, ATT_BAND, 2 * ATT_PAIR), F32),
                        pltpu.VMEM((ATT_SLOTS, ATT_BAND, 2 * ATT_PAIR), F32),
                        pltpu.VMEM((ATT_SLOTS, ATT_BAND, 2 * ATT_PAIR), BF16)],
        compiler_params=_params(("arbitrary", "arbitrary", "arbitrary")),
        name="band_attn",
    )(q, k, k, vt, vt, bias_t)


def _lane_col(x, idx):
    lane = lax.broadcasted_iota(jnp.int32, x.shape, 1)
    return jnp.sum(jnp.where(lane == idx, x, 0.0), axis=-1, keepdims=True)


def _wy_prep(q, k, v, beta, gc, eg, kef):
    c = CHUNK
    rows = q.shape[0]
    n = rows // c
    kb = k * beta
    qd = q * eg
    ke = k * kef
    rhs = jnp.concatenate([v * beta, kb * eg], axis=1).astype(BF16)
    kb16 = kb.astype(BF16)
    q16 = q.astype(BF16)
    k16 = k.astype(BF16)
    row = lax.broadcasted_iota(jnp.int32, (c, 128), 0)
    lane = lax.broadcasted_iota(jnp.int32, (c, 128), 1)
    col = lane & (c - 1)
    left = lane < c
    causal = row >= col
    strict = row > col
    eye_r = jnp.where(lane == row + c, 1.0, 0.0)
    sl = [slice(i * c, (i + 1) * c) for i in range(n)]
    sc = [_mm_nt(jnp.concatenate([kb16[s], q16[s]], axis=0),
                 jnp.concatenate([k16[s], k16[s]], axis=0)) for s in sl]
    ws, a_out = [], []
    for i, s in enumerate(sl):
        gci = jnp.broadcast_to(gc[s], (c, 128))
        gcol = jnp.sum(jnp.where(row == col, gci, 0.0), axis=0, keepdims=True)
        decay = jnp.where(causal, jnp.exp(jnp.where(causal, gci - gcol, 0.0)), 0.0)
        m = jnp.where(strict, sc[i][:c] * decay, 0.0)
        a_out.append((sc[i][c:] * decay)[:, :c])
        ws.append(jnp.where(left, -m, eye_r))
    for _ in range(6):
        rs = [_mm(w[:, :c].astype(BF16), w.astype(BF16)) for w in ws]
        ws = [r + jnp.where(left, 0.0, w) for r, w in zip(rs, ws)]
    uw = [_mm(w[:, c:].astype(BF16), rhs[s]) for w, s in zip(ws, sl)]
    return uw, a_out, qd, ke


def _a_pre_kernel(x_ref, halo_ref, g_ref, w_ref, wg_ref, cw_ref, hp_ref,
                  wq_ref, ke_ref, u_ref, a_ref, dec_ref, z_ref,
                  xn_ref, pj_ref, *, tm, tiles_per_seq):
    _fill_xn(xn_ref, x_ref[...], halo_ref[...], g_ref[...], pl.program_id(0), tiles_per_seq)
    nh, hd, c = A_HEADS, A_HEAD_DIM, CHUNK
    n = tm // c
    qk_scale = hd ** -0.5

    gr = _mm(xn_ref[HALO:, :], wg_ref[...])
    hp = hp_ref[...]
    beta_all = jax.nn.sigmoid(gr)
    a_in = gr + hp[1:2]
    g_all = -jnp.exp(hp[0:1]) * (jnp.maximum(a_in, 0.0) + jnp.log1p(jnp.exp(-jnp.abs(a_in))))
    rin = lax.broadcasted_iota(jnp.int32, (tm, 128), 0) & (c - 1)
    gc_all = g_all
    for s in (1, 2, 4, 8, 16, 32):
        gc_all = gc_all + jnp.where(rin >= s, pltpu.roll(gc_all, s, axis=0), 0.0)
    g_last = [gc_all[(i + 1) * c - 1:(i + 1) * c, :] for i in range(n)]
    eg_all = jnp.exp(gc_all)
    kef_all = jnp.exp(jnp.concatenate([jnp.broadcast_to(gl, (c, 128)) for gl in g_last], axis=0) - gc_all)
    dec_all = jnp.exp(jnp.concatenate(g_last, axis=0))

    def project(h):
        pj_ref[h % 2] = _mm(xn_ref[...], w_ref[h])

    project(0)
    for h in range(nh):
        if h + 1 < nh:
            project(h + 1)
        pj = pj_ref.at[h % 2]
        cw = cw_ref[h]
        qkv = (cw[0:1] * pj[pl.ds(HALO - 3, tm), 0:3 * hd]
               + cw[1:2] * pj[pl.ds(HALO - 2, tm), 0:3 * hd]
               + cw[2:3] * pj[pl.ds(HALO - 1, tm), 0:3 * hd]
               + cw[3:4] * pj[pl.ds(HALO, tm), 0:3 * hd])
        qkv = qkv * jax.nn.sigmoid(qkv)
        q = qkv[:, 0:hd]
        k = qkv[:, hd:2 * hd]
        v = qkv[:, 2 * hd:3 * hd]
        q = q * (lax.rsqrt(jnp.sum(q * q, axis=-1, keepdims=True) + EPS) * qk_scale)
        k = k * lax.rsqrt(jnp.sum(k * k, axis=-1, keepdims=True) + EPS)
        z_ref[h] = pj[pl.ds(HALO, tm), 3 * hd:4 * hd]
        uw, a_out, qd, ke = _wy_prep(q, k, v, _lane_col(beta_all, h), _lane_col(gc_all, nh + h),
                                     _lane_col(eg_all, nh + h), _lane_col(kef_all, nh + h))
        for i in range(n):
            r0 = i * c
            wq_ref[h, 2 * r0:2 * r0 + c, :] = uw[i][:, hd:].astype(BF16)
            wq_ref[h, 2 * r0 + c:2 * r0 + 2 * c, :] = qd[r0:r0 + c].astype(BF16)
            u_ref[h, r0:r0 + c, :] = uw[i][:, :hd]
            a_ref[h, r0:r0 + c, :] = a_out[i].astype(BF16)
        ke_ref[h] = ke.astype(BF16)
        dec_ref[0, h] = jnp.broadcast_to(_lane_col(dec_all, nh + h), (n, hd))


def _a_pre(h, g, w_in, conv_w, a_log, dt_bias, seq):
    t, d = h.shape
    nh, hd = A_HEADS, A_HEAD_DIM
    width = nh * hd
    tm = APRE_TM
    wcat = jnp.concatenate([w_in[:, s * width:(s + 1) * width].reshape(d, nh, hd) for s in range(4)],
                           axis=2).transpose(1, 0, 2).astype(BF16)
    wgate = jnp.pad(w_in[:, 4 * width:], ((0, 0), (0, 128 - 2 * nh))).astype(BF16)
    cw = jnp.concatenate([conv_w[:, s * width:(s + 1) * width].reshape(-1, nh, hd)
                          for s in range(3)], axis=2).transpose(1, 0, 2)
    hp = jnp.pad(jnp.stack([a_log, dt_bias], axis=0), ((0, 0), (nh, 128 - 2 * nh)))
    kern = functools.partial(_a_pre_kernel, tm=tm, tiles_per_seq=seq // tm)
    out_shape = (
        jax.ShapeDtypeStruct((nh, 2 * t, hd), BF16),
        jax.ShapeDtypeStruct((nh, t, hd), BF16),
        jax.ShapeDtypeStruct((nh, t, hd), F32),
        jax.ShapeDtypeStruct((nh, t, CHUNK), BF16),
        jax.ShapeDtypeStruct((t // tm, nh, DEC_ROWS, hd), F32),
        jax.ShapeDtypeStruct((nh, t, hd), F32),
    )
    return pl.pallas_call(
        kern,
        out_shape=out_shape,
        grid=(t // tm,),
        in_specs=[pl.BlockSpec((tm, d), lambda i: (i, 0)), _halo_spec(tm, d),
                  _const_spec((1, d)), _const_spec((nh, d, 4 * hd)), _const_spec((d, 128)),
                  _const_spec((nh, 4, 3 * hd)), _const_spec((2, 128))],
        out_specs=(
            pl.BlockSpec((nh, 2 * tm, hd), lambda i: (0, i, 0)),
            pl.BlockSpec((nh, tm, hd), lambda i: (0, i, 0)),
            pl.BlockSpec((nh, tm, hd), lambda i: (0, i, 0)),
            pl.BlockSpec((nh, tm, CHUNK), lambda i: (0, i, 0)),
            pl.BlockSpec((1, nh, DEC_ROWS, hd), lambda i: (i, 0, 0, 0)),
            pl.BlockSpec((nh, tm, hd), lambda i: (0, i, 0)),
        ),
        scratch_shapes=[pltpu.VMEM((tm + HALO, d), BF16), pltpu.VMEM((2, tm + HALO, 4 * hd), F32)],
        compiler_params=_params(("arbitrary",)),
        name="a_pre",
    )(h, h, g.reshape(1, d), wcat, wgate, cw, hp)


def _a_rec_kernel(wq_ref, ke_ref, u_ref, a_ref, dec_ref, z_ref, h_ref, onw_ref, wo_ref,
                  out_ref, state_ref, o_ref, y_ref, *, cb, nb):
    nh, hd = A_HEADS, A_HEAD_DIM
    rows = cb * CHUNK
    j = pl.program_id(1)

    @pl.when(j == 0)
    def _():
        state_ref[...] = jnp.zeros_like(state_ref)

    first = (j % (APRE_NCT // cb)) * cb
    chains = [(b, h) for b in range(nb) for h in range(nh)]

    def chunk(c, carry):
        r0 = pl.multiple_of(c * CHUNK, CHUNK)
        r1 = pl.multiple_of(c * 2 * CHUNK, 2 * CHUNK)
        st = [state_ref[b * nh + h] for b, h in chains]
        ws = [_mm(wq_ref[h, b, pl.ds(r1, 2 * CHUNK), :], s.astype(BF16))
              for (b, h), s in zip(chains, st)]
        v_new = [(u_ref[h, b, pl.ds(r0, CHUNK), :] - w[:CHUNK]).astype(BF16) for (b, h), w in zip(chains, ws)]
        av = [_mm(a_ref[h, b, pl.ds(r0, CHUNK), :], v) for (b, h), v in zip(chains, v_new)]
        upd = [_mm_tn(ke_ref[h, b, pl.ds(r0, CHUNK), :], v) for (b, h), v in zip(chains, v_new)]
        for i, (b, h) in enumerate(chains):
            o_ref[b, h, pl.ds(r0, CHUNK), :] = ws[i][CHUNK:] + av[i]
            dec = dec_ref[b, 0, h, pl.ds(first + c, 1), :]
            state_ref[b * nh + h] = st[i] * dec + upd[i]
        return carry

    lax.fori_loop(0, cb, chunk, 0)
    onw = onw_ref[...]
    for b in range(nb):
        for h in range(nh):
            z = z_ref[h, b]
            y_ref[b, :, h * hd:(h + 1) * hd] = (_rms(o_ref[b, h], onw) * (z * jax.nn.sigmoid(z))).astype(BF16)
    d = out_ref.shape[-1]
    out = h_ref[...].reshape(nb * rows, d) + _mm(y_ref[...].reshape(nb * rows, nh * hd), wo_ref[...])
    out_ref[...] = out.reshape(nb, rows, d)


def _a_rec(pre, h, out_norm_w, w_out, bsz, seq):
    wq, ke, u, a, dec, z = pre
    t, d = h.shape
    nh, hd = A_HEADS, A_HEAD_DIM
    cb, nb = AREC_CB, AREC_NB
    rows = cb * CHUNK
    nblk = seq // rows
    assert APRE_NCT % cb == 0 and bsz % nb == 0

    def by_seq(x, rows_per_seq):
        return x.reshape(nh, bsz, rows_per_seq, x.shape[-1])

    def idx(g, j):
        return (0, g, j, 0)

    out = pl.pallas_call(
        functools.partial(_a_rec_kernel, cb=cb, nb=nb),
        out_shape=jax.ShapeDtypeStruct((bsz, seq, d), F32),
        grid=(bsz // nb, nblk),
        in_specs=[pl.BlockSpec((nh, nb, 2 * rows, hd), idx), pl.BlockSpec((nh, nb, rows, hd), idx),
                  pl.BlockSpec((nh, nb, rows, hd), idx), pl.BlockSpec((nh, nb, rows, CHUNK), idx),
                  pl.BlockSpec((nb, 1, nh, DEC_ROWS, hd), lambda g, j: (g, j * cb // APRE_NCT, 0, 0, 0)),
                  pl.BlockSpec((nh, nb, rows, hd), idx),
                  pl.BlockSpec((nb, rows, d), lambda g, j: (g, j, 0)),
                  _const_spec((1, hd)), _const_spec((nh * hd, d))],
        out_specs=pl.BlockSpec((nb, rows, d), lambda g, j: (g, j, 0)),
        scratch_shapes=[pltpu.VMEM((nb * nh, hd, hd), F32), pltpu.VMEM((nb, nh, rows, hd), F32),
                        pltpu.VMEM((nb, rows, nh * hd), BF16)],
        compiler_params=_params(("arbitrary", "arbitrary")),
        name="a_rec",
    )(by_seq(wq, 2 * seq), by_seq(ke, seq), by_seq(u, seq), by_seq(a, seq),
      dec.reshape(bsz, seq // APRE_TM, nh, DEC_ROWS, hd), by_seq(z, seq), h.reshape(bsz, seq, d),
      out_norm_w.reshape(1, hd), w_out.astype(BF16))
    return out.reshape(t, d)


def kernel(x, a_norm, a_w_in, a_conv, a_A_log, a_dt_bias, a_out_norm, a_w_out, kv_norm, w_kv, b_norm, b_w_q, b_rel_bias, b_w_out, f_norm, f_w_up, f_conv, f_conv_b, f_w_down, final_norm):
    bsz, seq, d = x.shape
    n_a = a_norm.shape[0]
    n_b = b_norm.shape[0]
    depth = n_a + n_b
    h = x.reshape(bsz * seq, d)
    kv = None
    for layer in range(depth):
        if layer < n_a:
            i = layer
            pre = _a_pre(h, a_norm[i], a_w_in[i], a_conv[i], a_A_log[i], a_dt_bias[i], seq)
            h = _a_rec(pre, h, a_out_norm[i], a_w_out[i], bsz, seq)
        else:
            j = layer - n_a
            if kv is None:
                kv = _kv_proj(h, kv_norm, w_kv)
            q = _norm_proj(h, b_norm[j], b_w_q[j], B_HEAD_DIM ** -0.5 * LOG2E, "q_proj")
            o = _band_attn(q, kv[0], kv[1], b_rel_bias[j], bsz, seq)
            h = _proj_res(o, b_w_out[j], h)
        fg = final_norm if layer == depth - 1 else None
        h = _ffn(h, f_norm[layer], f_w_up[layer], f_conv[layer], f_conv_b[layer],
                 f_w_down[layer], fg, seq)
    return h.reshape(bsz, seq, d)
```

```python
import functools

import jax
import jax.numpy as jnp
from jax import lax
from jax.experimental import pallas as pl
from jax.experimental.pallas import tpu as pltpu

F32 = jnp.float32
BF16 = jnp.bfloat16
EPS = 1e-6
NEG_INF = -1e30
LOG2E = 1.4426950408889634

CHUNK = 64
LEFT_CHUNKS = 8
REL_CLIP = 256
A_HEADS = 8
A_HEAD_DIM = 128
B_HEAD_DIM = 64

HALO = 16
FFN_TM = 512
FFN_FC = 256
PROJ_TM = 512
APRE_TM = 512
APRE_NCT = APRE_TM // 64
DEC_ROWS = 8
assert APRE_NCT == DEC_ROWS
AREC_CB = 2
AREC_NB = 4
ATT_QB = 2048
VMEM_LIMIT = 56 * 1024 * 1024


def _rms(x, g):
    ms = jnp.mean(x * x, axis=-1, keepdims=True)
    return x * lax.rsqrt(ms + EPS) * g


def _mm(a, b):
    return jnp.dot(a, b, preferred_element_type=F32)


def _mm_nt(a, b):
    return lax.dot_general(a, b, (((1,), (1,)), ((), ())), preferred_element_type=F32)


def _mm_tn(a, b):
    return lax.dot_general(a, b, (((0,), (0,)), ((), ())), preferred_element_type=F32)


def _const_spec(shape):
    nd = len(shape)
    return pl.BlockSpec(shape, lambda *_: (0,) * nd, pipeline_mode=pl.Buffered(1))


def _params(sem):
    return pltpu.CompilerParams(dimension_semantics=sem, vmem_limit_bytes=VMEM_LIMIT)


def _halo_spec(tm, d):
    hb = tm // HALO
    return pl.BlockSpec((HALO, d), lambda i: (jnp.maximum(i * hb - 1, 0), 0))


def _fill_xn(xn_ref, x, halo, g, i, tiles_per_seq):
    xn_ref[HALO:, :] = _rms(x, g).astype(BF16)
    keep = (i % tiles_per_seq != 0).astype(F32)
    xn_ref[:HALO, :] = (_rms(halo, g) * keep).astype(BF16)


def _ffn_kernel(h_ref, y_ref, wo_ref, g_ref, wu_ref, cw_ref, wd_ref, fg_ref, o_ref,
                xn_ref, u_ref, act_ref, tail_ref, *, tm, fc, tiles_per_seq, final_norm):
    @pl.when(pl.program_id(0) % tiles_per_seq == 0)
    def _():
        tail_ref[...] = jnp.zeros_like(tail_ref)

    g = g_ref[...]
    x = h_ref[...] + _mm(y_ref[...], wo_ref[...])
    xn_ref[HALO:, :] = _rms(x, g).astype(BF16)
    xn_ref[:HALO, :] = _rms(tail_ref[...], g).astype(BF16)
    tail_ref[...] = x[tm - HALO:, :]
    f = wd_ref.shape[0]
    nch = f // fc

    def up(c):
        xn = xn_ref[...]
        for half in range(2):
            lo = half * f + c * fc
            u_ref[c % 2, half] = _mm(xn, wu_ref[:, lo:lo + fc])

    def conv(c, half):
        lo = half * f + c * fc
        cw = cw_ref[:, lo:lo + fc]
        u = u_ref.at[c % 2, half]
        return (cw[0:1] * u[pl.ds(HALO - 2, tm), :] + cw[1:2] * u[pl.ds(HALO - 1, tm), :]
                + cw[2:3] * u[pl.ds(HALO, tm), :] + cw[3:4])

    up(0)
    for c in range(nch):
        if c + 1 < nch:
            up(c + 1)
        gate = conv(c, 0)
        act_ref[:, c * fc:(c + 1) * fc] = (gate * jax.nn.sigmoid(gate) * conv(c, 1)).astype(BF16)
    out = x + _mm(act_ref[...], wd_ref[...])
    if final_norm:
        out = _rms(out, fg_ref[...])
    o_ref[...] = out


def _ffn(h, y, w_o, g, w_up, conv_w, conv_b, w_down, final_g, seq):
    t, d = h.shape
    k = y.shape[1]
    f = w_down.shape[0]
    tm, fc = FFN_TM, FFN_FC
    cw = jnp.concatenate([conv_w, conv_b[None, :]], axis=0)
    final_norm = final_g is not None
    fg = (final_g if final_norm else g).reshape(1, d)
    kern = functools.partial(_ffn_kernel, tm=tm, fc=fc, tiles_per_seq=seq // tm,
                             final_norm=final_norm)
    return pl.pallas_call(
        kern,
        out_shape=jax.ShapeDtypeStruct((t, d), F32),
        grid=(t // tm,),
        in_specs=[
            pl.BlockSpec((tm, d), lambda i: (i, 0)),
            pl.BlockSpec((tm, k), lambda i: (i, 0)),
            _const_spec((k, d)),
            _const_spec((1, d)),
            _const_spec((d, 2 * f)),
            _const_spec((4, 2 * f)),
            _const_spec((f, d)),
            _const_spec((1, d)),
        ],
        out_specs=pl.BlockSpec((tm, d), lambda i: (i, 0)),
        scratch_shapes=[
            pltpu.VMEM((tm + HALO, d), BF16),
            pltpu.VMEM((2, 2, tm + HALO, fc), F32),
            pltpu.VMEM((tm, f), BF16),
            pltpu.VMEM((HALO, d), F32),
        ],
        compiler_params=_params(("arbitrary",)),
        name="conv_ffn",
    )(h, y, w_o.astype(BF16), g.reshape(1, d), w_up.astype(BF16), cw, w_down.astype(BF16), fg)


def _norm_proj_kernel(x_ref, g_ref, w_ref, o_ref, *, scale):
    xn = _rms(x_ref[...], g_ref[...]).astype(BF16)
    o_ref[...] = (_mm(xn, w_ref[...]) * scale).astype(o_ref.dtype)


def _norm_proj(h, g, w, scale, name):
    t, d = h.shape
    n = w.shape[1]
    tm = PROJ_TM
    return pl.pallas_call(
        functools.partial(_norm_proj_kernel, scale=scale),
        out_shape=jax.ShapeDtypeStruct((t, n), BF16),
        grid=(t // tm,),
        in_specs=[pl.BlockSpec((tm, d), lambda i: (i, 0)), _const_spec((1, d)),
                  _const_spec((d, n))],
        out_specs=pl.BlockSpec((tm, n), lambda i: (i, 0)),
        compiler_params=_params(("arbitrary",)),
        name=name,
    )(h, g.reshape(1, d), w.astype(BF16))


def _kv_proj_kernel(x_ref, g_ref, wk_ref, wvt_ref, k_ref, vt_ref):
    xn = _rms(x_ref[...], g_ref[...]).astype(BF16)
    k_ref[...] = _mm(xn, wk_ref[...]).astype(k_ref.dtype)
    vt_ref[...] = _mm_nt(wvt_ref[...], xn).astype(vt_ref.dtype)


def _kv_proj(h, g, w_kv):
    t, d = h.shape
    width = w_kv.shape[1] // 2
    tm = PROJ_TM
    return pl.pallas_call(
        _kv_proj_kernel,
        out_shape=(jax.ShapeDtypeStruct((t, width), BF16), jax.ShapeDtypeStruct((width, t), BF16)),
        grid=(t // tm,),
        in_specs=[pl.BlockSpec((tm, d), lambda i: (i, 0)), _const_spec((1, d)),
                  _const_spec((d, width)), _const_spec((width, d))],
        out_specs=(pl.BlockSpec((tm, width), lambda i: (i, 0)),
                   pl.BlockSpec((width, tm), lambda i: (0, i))),
        compiler_params=_params(("arbitrary",)),
        name="kv_proj",
    )(h, g.reshape(1, d), w_kv[:, :width].astype(BF16), w_kv[:, width:].T.astype(BF16))


ATT_PAIR = 2 * CHUNK
ATT_BAND = (LEFT_CHUNKS + 2) * CHUNK
ATT_HIST = LEFT_CHUNKS * CHUNK
ATT_SLOTS = 8


def _attn_kernel(q_ref, kp_ref, kc_ref, vp_ref, vc_ref, b_ref, o_ref,
                 kbuf, vbuf, bhist, s_ref, e_ref, *, qb):
    i = pl.program_id(2)
    kbuf[0:ATT_HIST, :] = kp_ref[...]
    kbuf[ATT_HIST:, :] = kc_ref[...]
    vbuf[:, 0:ATT_HIST] = vp_ref[...]
    vbuf[:, ATT_HIST:] = vc_ref[...]
    left = lax.broadcasted_iota(jnp.int32, (ATT_PAIR, 128), 1) < B_HEAD_DIM
    zero = jnp.zeros((), BF16)
    nhist = ATT_HIST // ATT_PAIR
    hd = B_HEAD_DIM

    @pl.when(i == 0)
    def _():
        key = lax.broadcasted_iota(jnp.int32, (ATT_BAND, 1), 0)
        for p in range(nhist):
            bhist[p] = b_ref[0] + jnp.where(key < ATT_HIST - p * ATT_PAIR, NEG_INF, 0.0)

    @pl.when(i == 1)
    def _():
        for p in range(nhist):
            bhist[p] = b_ref[0]

    def scores(p):
        r0 = p * ATT_PAIR
        qp = q_ref[r0:r0 + ATT_PAIR, :]
        qq = jnp.concatenate([jnp.where(left, qp, zero), jnp.where(left, zero, qp)], axis=0)
        s_ref[p % ATT_SLOTS] = _mm_nt(kbuf[r0:r0 + ATT_BAND, :], qq)

    npair = qb // ATT_PAIR
    ahead = ATT_SLOTS - 1
    for p in range(min(ahead, npair)):
        scores(p)
    for p in range(npair):
        r0 = p * ATT_PAIR
        if p + ahead < npair:
            scores(p + ahead)
        b = bhist[p] if p < nhist else b_ref[0]
        m = jnp.max(s_ref[p % ATT_SLOTS] + b, axis=0, keepdims=True)
        e = jnp.exp2((s_ref[p % ATT_SLOTS] - m) + b)
        l = jnp.sum(e, axis=0, keepdims=True)
        e_ref[p % ATT_SLOTS] = e.astype(BF16)
        ot = _mm(vbuf[:, r0:r0 + ATT_BAND], e_ref[p % ATT_SLOTS]) / l
        ot = jnp.concatenate([ot[0:hd, 0:ATT_PAIR], ot[hd:2 * hd, ATT_PAIR:]], axis=0)
        o_ref[r0:r0 + ATT_PAIR, :] = ot.T.astype(o_ref.dtype)


def _band_bias(rel_bias):
    c = CHUNK
    band = (LEFT_CHUNKS + 1) * c
    nh = rel_bias.shape[0]
    assert c - 1 <= REL_CLIP < band - 1
    ramp = rel_bias[:, REL_CLIP - (c - 1):]
    g = jnp.concatenate([ramp, jnp.broadcast_to(rel_bias[:, -1:], (nh, band - 1 - REL_CLIP))], axis=1)
    period = band + c
    u = jnp.pad(g, ((0, 0), (0, period - g.shape[1])))
    hk = jnp.tile(u, (1, c + 1))[:, :c * (period + 1)].reshape(nh, c, period + 1)[:, :, :band]
    bias = hk[:, :, ::-1].astype(F32) * LOG2E
    top = jnp.pad(bias, ((0, 0), (0, 0), (0, CHUNK)), constant_values=NEG_INF)
    bot = jnp.pad(bias, ((0, 0), (0, 0), (CHUNK, 0)), constant_values=NEG_INF)
    return jnp.concatenate([top, bot], axis=1)


def _band_attn(q, k, vt, rel_bias, bsz, seq):
    t, width = q.shape
    nhp = width // 128
    qb = ATT_QB
    nq = seq // qb
    assert nq >= 2 and qb % ATT_HIST == 0
    hpb = qb // ATT_HIST
    bias_t = _band_bias(rel_bias).reshape(nhp, 2 * ATT_PAIR, ATT_BAND).transpose(0, 2, 1)

    def cur(b, hp, i):
        return (b * nq + i, hp)

    def prev(b, hp, i):
        return (jnp.maximum((b * nq + i) * hpb - 1, 0), hp)

    def vcur(b, hp, i):
        return (hp, b * nq + i)

    def vprev(b, hp, i):
        return (hp, jnp.maximum((b * nq + i) * hpb - 1, 0))

    return pl.pallas_call(
        functools.partial(_attn_kernel, qb=qb),
        out_shape=jax.ShapeDtypeStruct((t, width), BF16),
        grid=(bsz, nhp, nq),
        in_specs=[pl.BlockSpec((qb, 128), cur),
                  pl.BlockSpec((ATT_HIST, 128), prev), pl.BlockSpec((qb, 128), cur),
                  pl.BlockSpec((128, ATT_HIST), vprev), pl.BlockSpec((128, qb), vcur),
                  pl.BlockSpec((1, ATT_BAND, 2 * ATT_PAIR), lambda b, hp, i: (hp, 0, 0))],
        out_specs=pl.BlockSpec((qb, 128), cur),
        scratch_shapes=[pltpu.VMEM((ATT_HIST + qb, 128), BF16), pltpu.VMEM((128, ATT_HIST + qb), BF16),
                        pltpu.VMEM((ATT_HIST // ATT_PAIR, ATT_BAND, 2 * ATT_PAIR), F32),
                        pltpu.VMEM((ATT_SLOTS, ATT_BAND, 2 * ATT_PAIR), F32),
                        pltpu.VMEM((ATT_SLOTS, ATT_BAND, 2 * ATT_PAIR), BF16)],
        compiler_params=_params(("arbitrary", "arbitrary", "arbitrary")),
        name="band_attn",
    )(q, k, k, vt, vt, bias_t)


def _lane_col(x, idx):
    lane = lax.broadcasted_iota(jnp.int32, x.shape, 1)
    return jnp.sum(jnp.where(lane == idx, x, 0.0), axis=-1, keepdims=True)


def _wy_prep(q, k, v, beta, gc, eg, kef):
    c = CHUNK
    rows = q.shape[0]
    n = rows // c
    kb = k * beta
    qd = q * eg
    ke = k * kef
    rhs = jnp.concatenate([v * beta, kb * eg], axis=1).astype(BF16)
    kb16 = kb.astype(BF16)
    q16 = q.astype(BF16)
    k16 = k.astype(BF16)
    row = lax.broadcasted_iota(jnp.int32, (c, 128), 0)
    lane = lax.broadcasted_iota(jnp.int32, (c, 128), 1)
    col = lane & (c - 1)
    left = lane < c
    causal = row >= col
    strict = row > col
    eye_r = jnp.where(lane == row + c, 1.0, 0.0)
    sl = [slice(i * c, (i + 1) * c) for i in range(n)]
    sc = [_mm_nt(jnp.concatenate([kb16[s], q16[s]], axis=0),
                 jnp.concatenate([k16[s], k16[s]], axis=0)) for s in sl]
    ws, a_out = [], []
    for i, s in enumerate(sl):
        gci = jnp.broadcast_to(gc[s], (c, 128))
        gcol = jnp.sum(jnp.where(row == col, gci, 0.0), axis=0, keepdims=True)
        decay = jnp.where(causal, jnp.exp(jnp.where(causal, gci - gcol, 0.0)), 0.0)
        m = jnp.where(strict, sc[i][:c] * decay, 0.0)
        a_out.append((sc[i][c:] * decay)[:, :c])
        ws.append(jnp.where(left, -m, eye_r))
    for _ in range(6):
        rs = [_mm(w[:, :c].astype(BF16), w.astype(BF16)) for w in ws]
        ws = [r + jnp.where(left, 0.0, w) for r, w in zip(rs, ws)]
    uw = [_mm(w[:, c:].astype(BF16), rhs[s]) for w, s in zip(ws, sl)]
    return uw, a_out, qd, ke


def _a_pre_kernel(x_ref, halo_ref, g_ref, w_ref, wg_ref, cw_ref, hp_ref,
                  wq_ref, ke_ref, u_ref, a_ref, dec_ref, z_ref,
                  xn_ref, pj_ref, *, tm, tiles_per_seq):
    _fill_xn(xn_ref, x_ref[...], halo_ref[...], g_ref[...], pl.program_id(0), tiles_per_seq)
    nh, hd, c = A_HEADS, A_HEAD_DIM, CHUNK
    n = tm // c
    qk_scale = hd ** -0.5

    gr = _mm(xn_ref[HALO:, :], wg_ref[...])
    hp = hp_ref[...]
    beta_all = jax.nn.sigmoid(gr)
    a_in = gr + hp[1:2]
    g_all = -jnp.exp(hp[0:1]) * (jnp.maximum(a_in, 0.0) + jnp.log1p(jnp.exp(-jnp.abs(a_in))))
    rin = lax.broadcasted_iota(jnp.int32, (tm, 128), 0) & (c - 1)
    gc_all = g_all
    for s in (1, 2, 4, 8, 16, 32):
        gc_all = gc_all + jnp.where(rin >= s, pltpu.roll(gc_all, s, axis=0), 0.0)
    g_last = [gc_all[(i + 1) * c - 1:(i + 1) * c, :] for i in range(n)]
    eg_all = jnp.exp(gc_all)
    kef_all = jnp.exp(jnp.concatenate([jnp.broadcast_to(gl, (c, 128)) for gl in g_last], axis=0) - gc_all)
    dec_all = jnp.exp(jnp.concatenate(g_last, axis=0))

    def project(h):
        pj_ref[h % 2] = _mm(xn_ref[...], w_ref[h])

    project(0)
    for h in range(nh):
        if h + 1 < nh:
            project(h + 1)
        pj = pj_ref.at[h % 2]
        cw = cw_ref[h]
        qkv = (cw[0:1] * pj[pl.ds(HALO - 3, tm), 0:3 * hd]
               + cw[1:2] * pj[pl.ds(HALO - 2, tm), 0:3 * hd]
               + cw[2:3] * pj[pl.ds(HALO - 1, tm), 0:3 * hd]
               + cw[3:4] * pj[pl.ds(HALO, tm), 0:3 * hd])
        qkv = qkv * jax.nn.sigmoid(qkv)
        q = qkv[:, 0:hd]
        k = qkv[:, hd:2 * hd]
        v = qkv[:, 2 * hd:3 * hd]
        q = q * (lax.rsqrt(jnp.sum(q * q, axis=-1, keepdims=True) + EPS) * qk_scale)
        k = k * lax.rsqrt(jnp.sum(k * k, axis=-1, keepdims=True) + EPS)
        z_ref[h] = pj[pl.ds(HALO, tm), 3 * hd:4 * hd]
        uw, a_out, qd, ke = _wy_prep(q, k, v, _lane_col(beta_all, h), _lane_col(gc_all, nh + h),
                                     _lane_col(eg_all, nh + h), _lane_col(kef_all, nh + h))
        for i in range(n):
            r0 = i * c
            wq_ref[h, 2 * r0:2 * r0 + c, :] = uw[i][:, hd:].astype(BF16)
            wq_ref[h, 2 * r0 + c:2 * r0 + 2 * c, :] = qd[r0:r0 + c].astype(BF16)
            u_ref[h, r0:r0 + c, :] = uw[i][:, :hd]
            a_ref[h, r0:r0 + c, :] = a_out[i].astype(BF16)
        ke_ref[h] = ke.astype(BF16)
        dec_ref[0, h] = jnp.broadcast_to(_lane_col(dec_all, nh + h), (n, hd))


def _a_pre(h, g, w_in, conv_w, a_log, dt_bias, seq):
    t, d = h.shape
    nh, hd = A_HEADS, A_HEAD_DIM
    width = nh * hd
    tm = APRE_TM
    wcat = jnp.concatenate([w_in[:, s * width:(s + 1) * width].reshape(d, nh, hd) for s in range(4)],
                           axis=2).transpose(1, 0, 2).astype(BF16)
    wgate = jnp.pad(w_in[:, 4 * width:], ((0, 0), (0, 128 - 2 * nh))).astype(BF16)
    cw = jnp.concatenate([conv_w[:, s * width:(s + 1) * width].reshape(-1, nh, hd)
                          for s in range(3)], axis=2).transpose(1, 0, 2)
    hp = jnp.pad(jnp.stack([a_log, dt_bias], axis=0), ((0, 0), (nh, 128 - 2 * nh)))
    kern = functools.partial(_a_pre_kernel, tm=tm, tiles_per_seq=seq // tm)
    out_shape = (
        jax.ShapeDtypeStruct((nh, 2 * t, hd), BF16),
        jax.ShapeDtypeStruct((nh, t, hd), BF16),
        jax.ShapeDtypeStruct((nh, t, hd), F32),
        jax.ShapeDtypeStruct((nh, t, CHUNK), BF16),
        jax.ShapeDtypeStruct((t // tm, nh, DEC_ROWS, hd), F32),
        jax.ShapeDtypeStruct((nh, t, hd), F32),
    )
    return pl.pallas_call(
        kern,
        out_shape=out_shape,
        grid=(t // tm,),
        in_specs=[pl.BlockSpec((tm, d), lambda i: (i, 0)), _halo_spec(tm, d),
                  _const_spec((1, d)), _const_spec((nh, d, 4 * hd)), _const_spec((d, 128)),
                  _const_spec((nh, 4, 3 * hd)), _const_spec((2, 128))],
        out_specs=(
            pl.BlockSpec((nh, 2 * tm, hd), lambda i: (0, i, 0)),
            pl.BlockSpec((nh, tm, hd), lambda i: (0, i, 0)),
            pl.BlockSpec((nh, tm, hd), lambda i: (0, i, 0)),
            pl.BlockSpec((nh, tm, CHUNK), lambda i: (0, i, 0)),
            pl.BlockSpec((1, nh, DEC_ROWS, hd), lambda i: (i, 0, 0, 0)),
            pl.BlockSpec((nh, tm, hd), lambda i: (0, i, 0)),
        ),
        scratch_shapes=[pltpu.VMEM((tm + HALO, d), BF16), pltpu.VMEM((2, tm + HALO, 4 * hd), F32)],
        compiler_params=_params(("arbitrary",)),
        name="a_pre",
    )(h, h, g.reshape(1, d), wcat, wgate, cw, hp)


def _a_rec_kernel(wq_ref, ke_ref, u_ref, a_ref, dec_ref, z_ref, onw_ref,
                  y_ref, state_ref, o_ref, *, cb, nb):
    nh, hd = A_HEADS, A_HEAD_DIM
    rows = cb * CHUNK
    j = pl.program_id(1)

    @pl.when(j == 0)
    def _():
        state_ref[...] = jnp.zeros_like(state_ref)

    first = (j % (APRE_NCT // cb)) * cb
    chains = [(b, h) for b in range(nb) for h in range(nh)]

    def chunk(c, carry):
        r0 = pl.multiple_of(c * CHUNK, CHUNK)
        r1 = pl.multiple_of(c * 2 * CHUNK, 2 * CHUNK)
        st = [state_ref[b * nh + h] for b, h in chains]
        ws = [_mm(wq_ref[h, b, pl.ds(r1, 2 * CHUNK), :], s.astype(BF16))
              for (b, h), s in zip(chains, st)]
        v_new = [(u_ref[h, b, pl.ds(r0, CHUNK), :] - w[:CHUNK]).astype(BF16) for (b, h), w in zip(chains, ws)]
        av = [_mm(a_ref[h, b, pl.ds(r0, CHUNK), :], v) for (b, h), v in zip(chains, v_new)]
        upd = [_mm_tn(ke_ref[h, b, pl.ds(r0, CHUNK), :], v) for (b, h), v in zip(chains, v_new)]
        for i, (b, h) in enumerate(chains):
            o_ref[b, h, pl.ds(r0, CHUNK), :] = ws[i][CHUNK:] + av[i]
            dec = dec_ref[b, 0, h, pl.ds(first + c, 1), :]
            state_ref[b * nh + h] = st[i] * dec + upd[i]
        return carry

    lax.fori_loop(0, cb, chunk, 0)
    onw = onw_ref[...]
    for b in range(nb):
        for h in range(nh):
            z = z_ref[h, b]
            y_ref[b, :, h * hd:(h + 1) * hd] = (_rms(o_ref[b, h], onw) * (z * jax.nn.sigmoid(z))).astype(BF16)


def _a_rec(pre, out_norm_w, bsz, seq):
    wq, ke, u, a, dec, z = pre
    nh, hd = A_HEADS, A_HEAD_DIM
    cb, nb = AREC_CB, AREC_NB
    rows = cb * CHUNK
    nblk = seq // rows
    assert APRE_NCT % cb == 0 and bsz % nb == 0

    def by_seq(x, rows_per_seq):
        return x.reshape(nh, bsz, rows_per_seq, x.shape[-1])

    def idx(g, j):
        return (0, g, j, 0)

    y = pl.pallas_call(
        functools.partial(_a_rec_kernel, cb=cb, nb=nb),
        out_shape=jax.ShapeDtypeStruct((bsz, seq, nh * hd), BF16),
        grid=(bsz // nb, nblk),
        in_specs=[pl.BlockSpec((nh, nb, 2 * rows, hd), idx), pl.BlockSpec((nh, nb, rows, hd), idx),
                  pl.BlockSpec((nh, nb, rows, hd), idx), pl.BlockSpec((nh, nb, rows, CHUNK), idx),
                  pl.BlockSpec((nb, 1, nh, DEC_ROWS, hd), lambda g, j: (g, j * cb // APRE_NCT, 0, 0, 0)),
                  pl.BlockSpec((nh, nb, rows, hd), idx),
                  _const_spec((1, hd))],
        out_specs=pl.BlockSpec((nb, rows, nh * hd), lambda g, j: (g, j, 0)),
        scratch_shapes=[pltpu.VMEM((nb * nh, hd, hd), F32), pltpu.VMEM((nb, nh, rows, hd), F32)],
        compiler_params=_params(("arbitrary", "arbitrary")),
        name="a_rec",
    )(by_seq(wq, 2 * seq), by_seq(ke, seq), by_seq(u, seq), by_seq(a, seq),
      dec.reshape(bsz, seq // APRE_TM, nh, DEC_ROWS, hd), by_seq(z, seq), out_norm_w.reshape(1, hd))
    return y.reshape(bsz * seq, nh * hd)


def kernel(x, a_norm, a_w_in, a_conv, a_A_log, a_dt_bias, a_out_norm, a_w_out, kv_norm, w_kv, b_norm, b_w_q, b_rel_bias, b_w_out, f_norm, f_w_up, f_conv, f_conv_b, f_w_down, final_norm):
    bsz, seq, d = x.shape
    n_a = a_norm.shape[0]
    n_b = b_norm.shape[0]
    depth = n_a + n_b
    h = x.reshape(bsz * seq, d)
    kv = None
    for layer in range(depth):
        if layer < n_a:
            i = layer
            pre = _a_pre(h, a_norm[i], a_w_in[i], a_conv[i], a_A_log[i], a_dt_bias[i], seq)
            y, w_o = _a_rec(pre, a_out_norm[i], bsz, seq), a_w_out[i]
        else:
            j = layer - n_a
            if kv is None:
                kv = _kv_proj(h, kv_norm, w_kv)
            q = _norm_proj(h, b_norm[j], b_w_q[j], B_HEAD_DIM ** -0.5 * LOG2E, "q_proj")
            y, w_o = _band_attn(q, kv[0], kv[1], b_rel_bias[j], bsz, seq), b_w_out[j]
        fg = final_norm if layer == depth - 1 else None
        h = _ffn(h, y, w_o, f_norm[layer], f_w_up[layer], f_conv[layer], f_conv_b[layer],
                 f_w_down[layer], fg, seq)
    return h.reshape(bsz, seq, d)
```

```python
import functools

import jax
import jax.numpy as jnp
from jax import lax
from jax.experimental import pallas as pl
from jax.experimental.pallas import tpu as pltpu

F32 = jnp.float32
BF16 = jnp.bfloat16
EPS = 1e-6
NEG_INF = -1e30
LOG2E = 1.4426950408889634

CHUNK = 64
LEFT_CHUNKS = 8
REL_CLIP = 256
A_HEADS = 8
A_HEAD_DIM = 128
B_HEAD_DIM = 64

HALO = 16
FFN_TM = 512
FFN_FC = 256
PROJ_TM = 512
APRE_TM = 512
APRE_NCT = APRE_TM // 64
DEC_ROWS = 8
assert APRE_NCT == DEC_ROWS
AREC_CB = 2
AREC_NB = 4
ATT_QB = 4096
VMEM_LIMIT = 56 * 1024 * 1024


def _rms(x, g):
    ms = jnp.mean(x * x, axis=-1, keepdims=True)
    return x * lax.rsqrt(ms + EPS) * g


def _mm(a, b):
    return jnp.dot(a, b, preferred_element_type=F32)


def _mm_nt(a, b):
    return lax.dot_general(a, b, (((1,), (1,)), ((), ())), preferred_element_type=F32)


def _mm_tn(a, b):
    return lax.dot_general(a, b, (((0,), (0,)), ((), ())), preferred_element_type=F32)


def _const_spec(shape):
    nd = len(shape)
    return pl.BlockSpec(shape, lambda *_: (0,) * nd, pipeline_mode=pl.Buffered(1))


def _params(sem):
    return pltpu.CompilerParams(dimension_semantics=sem, vmem_limit_bytes=VMEM_LIMIT)


def _halo_spec(tm, d):
    hb = tm // HALO
    return pl.BlockSpec((HALO, d), lambda i: (jnp.maximum(i * hb - 1, 0), 0))


def _fill_xn(xn_ref, x, halo, g, i, tiles_per_seq):
    xn_ref[HALO:, :] = _rms(x, g).astype(BF16)
    keep = (i % tiles_per_seq != 0).astype(F32)
    xn_ref[:HALO, :] = (_rms(halo, g) * keep).astype(BF16)


def _ffn_kernel(h_ref, y_ref, wo_ref, g_ref, wu_ref, cw_ref, wd_ref, fg_ref, o_ref,
                xn_ref, u_ref, act_ref, tail_ref, *, tm, fc, tiles_per_seq, final_norm):
    @pl.when(pl.program_id(0) % tiles_per_seq == 0)
    def _():
        tail_ref[...] = jnp.zeros_like(tail_ref)

    g = g_ref[...]
    x = h_ref[...] + _mm(y_ref[...], wo_ref[...])
    xn_ref[HALO:, :] = _rms(x, g).astype(BF16)
    xn_ref[:HALO, :] = _rms(tail_ref[...], g).astype(BF16)
    tail_ref[...] = x[tm - HALO:, :]
    f = wd_ref.shape[0]
    nch = f // fc

    def up(c):
        xn = xn_ref[...]
        for half in range(2):
            lo = half * f + c * fc
            u_ref[c % 2, half] = _mm(xn, wu_ref[:, lo:lo + fc])

    def conv(c, half):
        lo = half * f + c * fc
        cw = cw_ref[:, lo:lo + fc]
        u = u_ref.at[c % 2, half]
        return (cw[0:1] * u[pl.ds(HALO - 2, tm), :] + cw[1:2] * u[pl.ds(HALO - 1, tm), :]
                + cw[2:3] * u[pl.ds(HALO, tm), :] + cw[3:4])

    up(0)
    for c in range(nch):
        if c + 1 < nch:
            up(c + 1)
        gate = conv(c, 0)
        act_ref[:, c * fc:(c + 1) * fc] = (gate * jax.nn.sigmoid(gate) * conv(c, 1)).astype(BF16)
    out = x + _mm(act_ref[...], wd_ref[...])
    if final_norm:
        out = _rms(out, fg_ref[...])
    o_ref[...] = out


def _ffn(h, y, w_o, g, w_up, conv_w, conv_b, w_down, final_g, seq):
    t, d = h.shape
    k = y.shape[1]
    f = w_down.shape[0]
    tm, fc = FFN_TM, FFN_FC
    cw = jnp.concatenate([conv_w, conv_b[None, :]], axis=0)
    final_norm = final_g is not None
    fg = (final_g if final_norm else g).reshape(1, d)
    kern = functools.partial(_ffn_kernel, tm=tm, fc=fc, tiles_per_seq=seq // tm,
                             final_norm=final_norm)
    return pl.pallas_call(
        kern,
        out_shape=jax.ShapeDtypeStruct((t, d), F32),
        grid=(t // tm,),
        in_specs=[
            pl.BlockSpec((tm, d), lambda i: (i, 0)),
            pl.BlockSpec((tm, k), lambda i: (i, 0)),
            _const_spec((k, d)),
            _const_spec((1, d)),
            _const_spec((d, 2 * f)),
            _const_spec((4, 2 * f)),
            _const_spec((f, d)),
            _const_spec((1, d)),
        ],
        out_specs=pl.BlockSpec((tm, d), lambda i: (i, 0)),
        scratch_shapes=[
            pltpu.VMEM((tm + HALO, d), BF16),
            pltpu.VMEM((2, 2, tm + HALO, fc), F32),
            pltpu.VMEM((tm, f), BF16),
            pltpu.VMEM((HALO, d), F32),
        ],
        compiler_params=_params(("arbitrary",)),
        name="conv_ffn",
    )(h, y, w_o.astype(BF16), g.reshape(1, d), w_up.astype(BF16), cw, w_down.astype(BF16), fg)


def _norm_proj_kernel(x_ref, g_ref, w_ref, o_ref, *, scale):
    xn = _rms(x_ref[...], g_ref[...]).astype(BF16)
    o_ref[...] = (_mm(xn, w_ref[...]) * scale).astype(o_ref.dtype)


def _norm_proj(h, g, w, scale, name):
    t, d = h.shape
    n = w.shape[1]
    tm = PROJ_TM
    return pl.pallas_call(
        functools.partial(_norm_proj_kernel, scale=scale),
        out_shape=jax.ShapeDtypeStruct((t, n), BF16),
        grid=(t // tm,),
        in_specs=[pl.BlockSpec((tm, d), lambda i: (i, 0)), _const_spec((1, d)),
                  _const_spec((d, n))],
        out_specs=pl.BlockSpec((tm, n), lambda i: (i, 0)),
        compiler_params=_params(("arbitrary",)),
        name=name,
    )(h, g.reshape(1, d), w.astype(BF16))


def _qkv_proj_kernel(x_ref, g_ref, wk_ref, wvt_ref, gq_ref, wq_ref, k_ref, vt_ref, q_ref, *, scale):
    x = x_ref[...]
    xhat = x * lax.rsqrt(jnp.mean(x * x, axis=-1, keepdims=True) + EPS)
    xn = (xhat * g_ref[...]).astype(BF16)
    k_ref[...] = _mm(xn, wk_ref[...]).astype(k_ref.dtype)
    vt_ref[...] = _mm_nt(wvt_ref[...], xn).astype(vt_ref.dtype)
    q_ref[...] = (_mm((xhat * gq_ref[...]).astype(BF16), wq_ref[...]) * scale).astype(q_ref.dtype)


def _qkv_proj(h, g, w_kv, gq, w_q, scale):
    t, d = h.shape
    width = w_kv.shape[1] // 2
    tm = PROJ_TM
    return pl.pallas_call(
        functools.partial(_qkv_proj_kernel, scale=scale),
        out_shape=(jax.ShapeDtypeStruct((t, width), BF16), jax.ShapeDtypeStruct((width, t), BF16),
                   jax.ShapeDtypeStruct((t, w_q.shape[1]), BF16)),
        grid=(t // tm,),
        in_specs=[pl.BlockSpec((tm, d), lambda i: (i, 0)), _const_spec((1, d)),
                  _const_spec((d, width)), _const_spec((width, d)),
                  _const_spec((1, d)), _const_spec((d, w_q.shape[1]))],
        out_specs=(pl.BlockSpec((tm, width), lambda i: (i, 0)),
                   pl.BlockSpec((width, tm), lambda i: (0, i)),
                   pl.BlockSpec((tm, w_q.shape[1]), lambda i: (i, 0))),
        compiler_params=_params(("arbitrary",)),
        name="qkv_proj",
    )(h, g.reshape(1, d), w_kv[:, :width].astype(BF16), w_kv[:, width:].T.astype(BF16),
      gq.reshape(1, d), w_q.astype(BF16))


ATT_PAIR = 2 * CHUNK
ATT_BAND = (LEFT_CHUNKS + 2) * CHUNK
ATT_HIST = LEFT_CHUNKS * CHUNK
ATT_SLOTS = 8


def _attn_kernel(q_ref, kp_ref, kc_ref, vp_ref, vc_ref, b_ref, o_ref,
                 kbuf, vbuf, bhist, s_ref, e_ref, *, qb):
    i = pl.program_id(2)
    kbuf[0:ATT_HIST, :] = kp_ref[...]
    kbuf[ATT_HIST:, :] = kc_ref[...]
    vbuf[:, 0:ATT_HIST] = vp_ref[...]
    vbuf[:, ATT_HIST:] = vc_ref[...]
    left = lax.broadcasted_iota(jnp.int32, (ATT_PAIR, 128), 1) < B_HEAD_DIM
    zero = jnp.zeros((), BF16)
    nhist = ATT_HIST // ATT_PAIR
    hd = B_HEAD_DIM

    @pl.when(i == 0)
    def _():
        key = lax.broadcasted_iota(jnp.int32, (ATT_BAND, 1), 0)
        for p in range(nhist):
            bhist[p] = b_ref[0] + jnp.where(key < ATT_HIST - p * ATT_PAIR, NEG_INF, 0.0)

    @pl.when(i == 1)
    def _():
        for p in range(nhist):
            bhist[p] = b_ref[0]

    def scores(p):
        r0 = p * ATT_PAIR
        qp = q_ref[r0:r0 + ATT_PAIR, :]
        qq = jnp.concatenate([jnp.where(left, qp, zero), jnp.where(left, zero, qp)], axis=0)
        s_ref[p % ATT_SLOTS] = _mm_nt(kbuf[r0:r0 + ATT_BAND, :], qq)

    npair = qb // ATT_PAIR
    ahead = ATT_SLOTS - 1
    for p in range(min(ahead, npair)):
        scores(p)
    for p in range(npair):
        r0 = p * ATT_PAIR
        if p + ahead < npair:
            scores(p + ahead)
        b = bhist[p] if p < nhist else b_ref[0]
        m = jnp.max(s_ref[p % ATT_SLOTS] + b, axis=0, keepdims=True)
        e = jnp.exp2((s_ref[p % ATT_SLOTS] - m) + b)
        l = jnp.sum(e, axis=0, keepdims=True)
        e_ref[p % ATT_SLOTS] = e.astype(BF16)
        ot = _mm(vbuf[:, r0:r0 + ATT_BAND], e_ref[p % ATT_SLOTS]) / l
        ot = jnp.concatenate([ot[0:hd, 0:ATT_PAIR], ot[hd:2 * hd, ATT_PAIR:]], axis=0)
        o_ref[r0:r0 + ATT_PAIR, :] = ot.T.astype(o_ref.dtype)


def _band_bias(rel_bias):
    c = CHUNK
    band = (LEFT_CHUNKS + 1) * c
    nh = rel_bias.shape[0]
    assert c - 1 <= REL_CLIP < band - 1
    ramp = rel_bias[:, REL_CLIP - (c - 1):]
    g = jnp.concatenate([ramp, jnp.broadcast_to(rel_bias[:, -1:], (nh, band - 1 - REL_CLIP))], axis=1)
    period = band + c
    u = jnp.pad(g, ((0, 0), (0, period - g.shape[1])))
    hk = jnp.tile(u, (1, c + 1))[:, :c * (period + 1)].reshape(nh, c, period + 1)[:, :, :band]
    bias = hk[:, :, ::-1].astype(F32) * LOG2E
    top = jnp.pad(bias, ((0, 0), (0, 0), (0, CHUNK)), constant_values=NEG_INF)
    bot = jnp.pad(bias, ((0, 0), (0, 0), (CHUNK, 0)), constant_values=NEG_INF)
    return jnp.concatenate([top, bot], axis=1)


def _band_attn(q, k, vt, rel_bias, bsz, seq):
    t, width = q.shape
    nhp = width // 128
    qb = ATT_QB
    nq = seq // qb
    assert nq >= 2 and qb % ATT_HIST == 0
    hpb = qb // ATT_HIST
    bias_t = _band_bias(rel_bias).reshape(nhp, 2 * ATT_PAIR, ATT_BAND).transpose(0, 2, 1)

    def cur(b, hp, i):
        return (b * nq + i, hp)

    def prev(b, hp, i):
        return (jnp.maximum((b * nq + i) * hpb - 1, 0), hp)

    def vcur(b, hp, i):
        return (hp, b * nq + i)

    def vprev(b, hp, i):
        return (hp, jnp.maximum((b * nq + i) * hpb - 1, 0))

    return pl.pallas_call(
        functools.partial(_attn_kernel, qb=qb),
        out_shape=jax.ShapeDtypeStruct((t, width), BF16),
        grid=(bsz, nhp, nq),
        in_specs=[pl.BlockSpec((qb, 128), cur),
                  pl.BlockSpec((ATT_HIST, 128), prev), pl.BlockSpec((qb, 128), cur),
                  pl.BlockSpec((128, ATT_HIST), vprev), pl.BlockSpec((128, qb), vcur),
                  pl.BlockSpec((1, ATT_BAND, 2 * ATT_PAIR), lambda b, hp, i: (hp, 0, 0))],
        out_specs=pl.BlockSpec((qb, 128), cur),
        scratch_shapes=[pltpu.VMEM((ATT_HIST + qb, 128), BF16), pltpu.VMEM((128, ATT_HIST + qb), BF16),
                        pltpu.VMEM((ATT_HIST // ATT_PAIR, ATT_BAND, 2 * ATT_PAIR), F32),
                        pltpu.VMEM((ATT_SLOTS, ATT_BAND, 2 * ATT_PAIR), F32),
                        pltpu.VMEM((ATT_SLOTS, ATT_BAND, 2 * ATT_PAIR), BF16)],
        compiler_params=_params(("arbitrary", "arbitrary", "arbitrary")),
        name="band_attn",
    )(q, k, k, vt, vt, bias_t)


def _lane_col(x, idx):
    lane = lax.broadcasted_iota(jnp.int32, x.shape, 1)
    return jnp.sum(jnp.where(lane == idx, x, 0.0), axis=-1, keepdims=True)


def _wy_prep(q, k, v, beta, gc, eg, kef):
    c = CHUNK
    rows = q.shape[0]
    n = rows // c
    kb = k * beta
    qd = q * eg
    ke = k * kef
    rhs = jnp.concatenate([v * beta, kb * eg], axis=1).astype(BF16)
    kb16 = kb.astype(BF16)
    q16 = q.astype(BF16)
    k16 = k.astype(BF16)
    row = lax.broadcasted_iota(jnp.int32, (c, 128), 0)
    lane = lax.broadcasted_iota(jnp.int32, (c, 128), 1)
    col = lane & (c - 1)
    left = lane < c
    causal = row >= col
    strict = row > col
    eye_r = jnp.where(lane == row + c, 1.0, 0.0)
    sl = [slice(i * c, (i + 1) * c) for i in range(n)]
    sc = [_mm_nt(jnp.concatenate([kb16[s], q16[s]], axis=0),
                 jnp.concatenate([k16[s], k16[s]], axis=0)) for s in sl]
    ws, a_out = [], []
    for i, s in enumerate(sl):
        gci = jnp.broadcast_to(gc[s], (c, 128))
        gcol = jnp.sum(jnp.where(row == col, gci, 0.0), axis=0, keepdims=True)
        decay = jnp.where(causal, jnp.exp(jnp.where(causal, gci - gcol, 0.0)), 0.0)
        m = jnp.where(strict, sc[i][:c] * decay, 0.0)
        a_out.append((sc[i][c:] * decay)[:, :c])
        ws.append(jnp.where(left, -m, eye_r))
    for _ in range(6):
        rs = [_mm(w[:, :c].astype(BF16), w.astype(BF16)) for w in ws]
        ws = [r + jnp.where(left, 0.0, w) for r, w in zip(rs, ws)]
    uw = [_mm(w[:, c:].astype(BF16), rhs[s]) for w, s in zip(ws, sl)]
    return uw, a_out, qd, ke


def _a_pre_kernel(x_ref, halo_ref, g_ref, w_ref, wg_ref, cw_ref, hp_ref,
                  wq_ref, ke_ref, u_ref, a_ref, dec_ref, z_ref,
                  xn_ref, pj_ref, *, tm, tiles_per_seq):
    _fill_xn(xn_ref, x_ref[...], halo_ref[...], g_ref[...], pl.program_id(0), tiles_per_seq)
    nh, hd, c = A_HEADS, A_HEAD_DIM, CHUNK
    n = tm // c
    qk_scale = hd ** -0.5

    gr = _mm(xn_ref[HALO:, :], wg_ref[...])
    hp = hp_ref[...]
    beta_all = jax.nn.sigmoid(gr)
    a_in = gr + hp[1:2]
    g_all = -jnp.exp(hp[0:1]) * (jnp.maximum(a_in, 0.0) + jnp.log1p(jnp.exp(-jnp.abs(a_in))))
    rin = lax.broadcasted_iota(jnp.int32, (tm, 128), 0) & (c - 1)
    gc_all = g_all
    for s in (1, 2, 4, 8, 16, 32):
        gc_all = gc_all + jnp.where(rin >= s, pltpu.roll(gc_all, s, axis=0), 0.0)
    g_last = [gc_all[(i + 1) * c - 1:(i + 1) * c, :] for i in range(n)]
    eg_all = jnp.exp(gc_all)
    kef_all = jnp.exp(jnp.concatenate([jnp.broadcast_to(gl, (c, 128)) for gl in g_last], axis=0) - gc_all)
    dec_all = jnp.exp(jnp.concatenate(g_last, axis=0))

    def project(h):
        pj_ref[h % 2] = _mm(xn_ref[...], w_ref[h])

    project(0)
    for h in range(nh):
        if h + 1 < nh:
            project(h + 1)
        pj = pj_ref.at[h % 2]
        cw = cw_ref[h]
        qkv = (cw[0:1] * pj[pl.ds(HALO - 3, tm), 0:3 * hd]
               + cw[1:2] * pj[pl.ds(HALO - 2, tm), 0:3 * hd]
               + cw[2:3] * pj[pl.ds(HALO - 1, tm), 0:3 * hd]
               + cw[3:4] * pj[pl.ds(HALO, tm), 0:3 * hd])
        qkv = qkv * jax.nn.sigmoid(qkv)
        q = qkv[:, 0:hd]
        k = qkv[:, hd:2 * hd]
        v = qkv[:, 2 * hd:3 * hd]
        q = q * (lax.rsqrt(jnp.sum(q * q, axis=-1, keepdims=True) + EPS) * qk_scale)
        k = k * lax.rsqrt(jnp.sum(k * k, axis=-1, keepdims=True) + EPS)
        z_ref[h] = pj[pl.ds(HALO, tm), 3 * hd:4 * hd]
        uw, a_out, qd, ke = _wy_prep(q, k, v, _lane_col(beta_all, h), _lane_col(gc_all, nh + h),
                                     _lane_col(eg_all, nh + h), _lane_col(kef_all, nh + h))
        for i in range(n):
            r0 = i * c
            wq_ref[h, 2 * r0:2 * r0 + c, :] = uw[i][:, hd:].astype(BF16)
            wq_ref[h, 2 * r0 + c:2 * r0 + 2 * c, :] = qd[r0:r0 + c].astype(BF16)
            u_ref[h, r0:r0 + c, :] = uw[i][:, :hd]
            a_ref[h, r0:r0 + c, :] = a_out[i].astype(BF16)
        ke_ref[h] = ke.astype(BF16)
        dec_ref[0, h] = jnp.broadcast_to(_lane_col(dec_all, nh + h), (n, hd))


def _a_pre(h, g, w_in, conv_w, a_log, dt_bias, seq):
    t, d = h.shape
    nh, hd = A_HEADS, A_HEAD_DIM
    width = nh * hd
    tm = APRE_TM
    wcat = jnp.concatenate([w_in[:, s * width:(s + 1) * width].reshape(d, nh, hd) for s in range(4)],
                           axis=2).transpose(1, 0, 2).astype(BF16)
    wgate = jnp.pad(w_in[:, 4 * width:], ((0, 0), (0, 128 - 2 * nh))).astype(BF16)
    cw = jnp.concatenate([conv_w[:, s * width:(s + 1) * width].reshape(-1, nh, hd)
                          for s in range(3)], axis=2).transpose(1, 0, 2)
    hp = jnp.pad(jnp.stack([a_log, dt_bias], axis=0), ((0, 0), (nh, 128 - 2 * nh)))
    kern = functools.partial(_a_pre_kernel, tm=tm, tiles_per_seq=seq // tm)
    out_shape = (
        jax.ShapeDtypeStruct((nh, 2 * t, hd), BF16),
        jax.ShapeDtypeStruct((nh, t, hd), BF16),
        jax.ShapeDtypeStruct((nh, t, hd), F32),
        jax.ShapeDtypeStruct((nh, t, CHUNK), BF16),
        jax.ShapeDtypeStruct((t // tm, nh, DEC_ROWS, hd), F32),
        jax.ShapeDtypeStruct((nh, t, hd), F32),
    )
    return pl.pallas_call(
        kern,
        out_shape=out_shape,
        grid=(t // tm,),
        in_specs=[pl.BlockSpec((tm, d), lambda i: (i, 0)), _halo_spec(tm, d),
                  _const_spec((1, d)), _const_spec((nh, d, 4 * hd)), _const_spec((d, 128)),
                  _const_spec((nh, 4, 3 * hd)), _const_spec((2, 128))],
        out_specs=(
            pl.BlockSpec((nh, 2 * tm, hd), lambda i: (0, i, 0)),
            pl.BlockSpec((nh, tm, hd), lambda i: (0, i, 0)),
            pl.BlockSpec((nh, tm, hd), lambda i: (0, i, 0)),
            pl.BlockSpec((nh, tm, CHUNK), lambda i: (0, i, 0)),
            pl.BlockSpec((1, nh, DEC_ROWS, hd), lambda i: (i, 0, 0, 0)),
            pl.BlockSpec((nh, tm, hd), lambda i: (0, i, 0)),
        ),
        scratch_shapes=[pltpu.VMEM((tm + HALO, d), BF16), pltpu.VMEM((2, tm + HALO, 4 * hd), F32)],
        compiler_params=_params(("arbitrary",)),
        name="a_pre",
    )(h, h, g.reshape(1, d), wcat, wgate, cw, hp)


def _a_rec_kernel(wq_ref, ke_ref, u_ref, a_ref, dec_ref, z_ref, onw_ref,
                  y_ref, state_ref, o_ref, *, cb, nb):
    nh, hd = A_HEADS, A_HEAD_DIM
    rows = cb * CHUNK
    j = pl.program_id(1)

    @pl.when(j == 0)
    def _():
        state_ref[...] = jnp.zeros_like(state_ref)

    first = (j % (APRE_NCT // cb)) * cb
    chains = [(b, h) for b in range(nb) for h in range(nh)]

    def chunk(c, carry):
        r0 = pl.multiple_of(c * CHUNK, CHUNK)
        r1 = pl.multiple_of(c * 2 * CHUNK, 2 * CHUNK)
        st = [state_ref[b * nh + h] for b, h in chains]
        ws = [_mm(wq_ref[h, b, pl.ds(r1, 2 * CHUNK), :], s.astype(BF16))
              for (b, h), s in zip(chains, st)]
        v_new = [(u_ref[h, b, pl.ds(r0, CHUNK), :] - w[:CHUNK]).astype(BF16) for (b, h), w in zip(chains, ws)]
        av = [_mm(a_ref[h, b, pl.ds(r0, CHUNK), :], v) for (b, h), v in zip(chains, v_new)]
        upd = [_mm_tn(ke_ref[h, b, pl.ds(r0, CHUNK), :], v) for (b, h), v in zip(chains, v_new)]
        for i, (b, h) in enumerate(chains):
            o_ref[b, h, pl.ds(r0, CHUNK), :] = ws[i][CHUNK:] + av[i]
            dec = dec_ref[b, 0, h, pl.ds(first + c, 1), :]
            state_ref[b * nh + h] = st[i] * dec + upd[i]
        return carry

    lax.fori_loop(0, cb, chunk, 0)
    onw = onw_ref[...]
    for b in range(nb):
        for h in range(nh):
            z = z_ref[h, b]
            y_ref[b, :, h * hd:(h + 1) * hd] = (_rms(o_ref[b, h], onw) * (z * jax.nn.sigmoid(z))).astype(BF16)


def _a_rec(pre, out_norm_w, bsz, seq):
    wq, ke, u, a, dec, z = pre
    nh, hd = A_HEADS, A_HEAD_DIM
    cb, nb = AREC_CB, AREC_NB
    rows = cb * CHUNK
    nblk = seq // rows
    assert APRE_NCT % cb == 0 and bsz % nb == 0

    def by_seq(x, rows_per_seq):
        return x.reshape(nh, bsz, rows_per_seq, x.shape[-1])

    def idx(g, j):
        return (0, g, j, 0)

    y = pl.pallas_call(
        functools.partial(_a_rec_kernel, cb=cb, nb=nb),
        out_shape=jax.ShapeDtypeStruct((bsz, seq, nh * hd), BF16),
        grid=(bsz // nb, nblk),
        in_specs=[pl.BlockSpec((nh, nb, 2 * rows, hd), idx), pl.BlockSpec((nh, nb, rows, hd), idx),
                  pl.BlockSpec((nh, nb, rows, hd), idx), pl.BlockSpec((nh, nb, rows, CHUNK), idx),
                  pl.BlockSpec((nb, 1, nh, DEC_ROWS, hd), lambda g, j: (g, j * cb // APRE_NCT, 0, 0, 0)),
                  pl.BlockSpec((nh, nb, rows, hd), idx),
                  _const_spec((1, hd))],
        out_specs=pl.BlockSpec((nb, rows, nh * hd), lambda g, j: (g, j, 0)),
        scratch_shapes=[pltpu.VMEM((nb * nh, hd, hd), F32), pltpu.VMEM((nb, nh, rows, hd), F32)],
        compiler_params=_params(("arbitrary", "arbitrary")),
        name="a_rec",
    )(by_seq(wq, 2 * seq), by_seq(ke, seq), by_seq(u, seq), by_seq(a, seq),
      dec.reshape(bsz, seq // APRE_TM, nh, DEC_ROWS, hd), by_seq(z, seq), out_norm_w.reshape(1, hd))
    return y.reshape(bsz * seq, nh * hd)


def kernel(x, a_norm, a_w_in, a_conv, a_A_log, a_dt_bias, a_out_norm, a_w_out, kv_norm, w_kv, b_norm, b_w_q, b_rel_bias, b_w_out, f_norm, f_w_up, f_conv, f_conv_b, f_w_down, final_norm):
    bsz, seq, d = x.shape
    n_a = a_norm.shape[0]
    n_b = b_norm.shape[0]
    depth = n_a + n_b
    h = x.reshape(bsz * seq, d)
    kv = None
    for layer in range(depth):
        if layer < n_a:
            i = layer
            pre = _a_pre(h, a_norm[i], a_w_in[i], a_conv[i], a_A_log[i], a_dt_bias[i], seq)
            y, w_o = _a_rec(pre, a_out_norm[i], bsz, seq), a_w_out[i]
        else:
            j = layer - n_a
            q_scale = B_HEAD_DIM ** -0.5 * LOG2E
            if kv is None:
                k, vt, q = _qkv_proj(h, kv_norm, w_kv, b_norm[j], b_w_q[j], q_scale)
                kv = (k, vt)
            else:
                q = _norm_proj(h, b_norm[j], b_w_q[j], q_scale, "q_proj")
            y, w_o = _band_attn(q, kv[0], kv[1], b_rel_bias[j], bsz, seq), b_w_out[j]
        fg = final_norm if layer == depth - 1 else None
        h = _ffn(h, y, w_o, f_norm[layer], f_w_up[layer], f_conv[layer], f_conv_b[layer],
                 f_w_down[layer], fg, seq)
    return h.reshape(bsz, seq, d)
```

```python
import functools

import jax
import jax.numpy as jnp
from jax import lax
from jax.experimental import pallas as pl
from jax.experimental.pallas import tpu as pltpu

F32 = jnp.float32
BF16 = jnp.bfloat16
EPS = 1e-6
NEG_INF = -1e30
LOG2E = 1.4426950408889634

CHUNK = 64
LEFT_CHUNKS = 8
REL_CLIP = 256
A_HEADS = 8
A_HEAD_DIM = 128
B_HEAD_DIM = 64

HALO = 16
FFN_TM = 512
FFN_FC = 256
PROJ_TM = 512
APRE_TM = 512
APRE_NCT = APRE_TM // 64
DEC_ROWS = 8
assert APRE_NCT == DEC_ROWS
AREC_CB = 2
AREC_NB = 4
ATT_QB = 4096
VMEM_LIMIT = 56 * 1024 * 1024


def _rms(x, g):
    ms = jnp.mean(x * x, axis=-1, keepdims=True)
    return x * lax.rsqrt(ms + EPS) * g


def _mm(a, b):
    return jnp.dot(a, b, preferred_element_type=F32)


def _mm_nt(a, b):
    return lax.dot_general(a, b, (((1,), (1,)), ((), ())), preferred_element_type=F32)


def _mm_tn(a, b):
    return lax.dot_general(a, b, (((0,), (0,)), ((), ())), preferred_element_type=F32)


def _const_spec(shape):
    nd = len(shape)
    return pl.BlockSpec(shape, lambda *_: (0,) * nd, pipeline_mode=pl.Buffered(1))


def _params(sem):
    return pltpu.CompilerParams(dimension_semantics=sem, vmem_limit_bytes=VMEM_LIMIT)


def _halo_spec(tm, d):
    hb = tm // HALO
    return pl.BlockSpec((HALO, d), lambda i: (jnp.maximum(i * hb - 1, 0), 0))


def _fill_xn(xn_ref, x, halo, g, i, tiles_per_seq):
    xn_ref[HALO:, :] = _rms(x, g).astype(BF16)
    keep = (i % tiles_per_seq != 0).astype(F32)
    xn_ref[:HALO, :] = (_rms(halo, g) * keep).astype(BF16)


def _ffn_kernel(h_ref, y_ref, wo_ref, g_ref, wu_ref, cw_ref, wd_ref, fg_ref, o_ref,
                xn_ref, u_ref, act_ref, tail_ref, *, tm, fc, tiles_per_seq, final_norm):
    @pl.when(pl.program_id(0) % tiles_per_seq == 0)
    def _():
        tail_ref[...] = jnp.zeros_like(tail_ref)

    g = g_ref[...]
    x = h_ref[...] + _mm(y_ref[...], wo_ref[...])
    xn_ref[HALO:, :] = _rms(x, g).astype(BF16)
    xn_ref[:HALO, :] = _rms(tail_ref[...], g).astype(BF16)
    tail_ref[...] = x[tm - HALO:, :]
    f = wd_ref.shape[0]
    nch = f // fc

    def up(c):
        xn = xn_ref[...]
        for half in range(2):
            lo = half * f + c * fc
            u_ref[c % 2, half] = _mm(xn, wu_ref[:, lo:lo + fc])

    def conv(c, half):
        lo = half * f + c * fc
        cw = cw_ref[:, lo:lo + fc]
        u = u_ref[c % 2, half]
        hor = cw[0:1] * u
        for tap in (1, 2):
            hor = pltpu.roll(hor, 1, axis=0) + cw[tap:tap + 1] * u
        return hor[HALO:, :] + cw[3:4]

    up(0)
    for c in range(nch):
        if c + 1 < nch:
            up(c + 1)
        gate = conv(c, 0)
        act_ref[:, c * fc:(c + 1) * fc] = (gate * jax.nn.sigmoid(gate) * conv(c, 1)).astype(BF16)
    out = x + _mm(act_ref[...], wd_ref[...])
    if final_norm:
        out = _rms(out, fg_ref[...])
    o_ref[...] = out


def _ffn(h, y, w_o, g, w_up, conv_w, conv_b, w_down, final_g, seq):
    t, d = h.shape
    k = y.shape[1]
    f = w_down.shape[0]
    tm, fc = FFN_TM, FFN_FC
    cw = jnp.concatenate([conv_w, conv_b[None, :]], axis=0)
    final_norm = final_g is not None
    fg = (final_g if final_norm else g).reshape(1, d)
    kern = functools.partial(_ffn_kernel, tm=tm, fc=fc, tiles_per_seq=seq // tm,
                             final_norm=final_norm)
    return pl.pallas_call(
        kern,
        out_shape=jax.ShapeDtypeStruct((t, d), F32),
        grid=(t // tm,),
        in_specs=[
            pl.BlockSpec((tm, d), lambda i: (i, 0)),
            pl.BlockSpec((tm, k), lambda i: (i, 0)),
            _const_spec((k, d)),
            _const_spec((1, d)),
            _const_spec((d, 2 * f)),
            _const_spec((4, 2 * f)),
            _const_spec((f, d)),
            _const_spec((1, d)),
        ],
        out_specs=pl.BlockSpec((tm, d), lambda i: (i, 0)),
        scratch_shapes=[
            pltpu.VMEM((tm + HALO, d), BF16),
            pltpu.VMEM((2, 2, tm + HALO, fc), F32),
            pltpu.VMEM((tm, f), BF16),
            pltpu.VMEM((HALO, d), F32),
        ],
        compiler_params=_params(("arbitrary",)),
        name="conv_ffn",
    )(h, y, w_o.astype(BF16), g.reshape(1, d), w_up.astype(BF16), cw, w_down.astype(BF16), fg)


def _norm_proj_kernel(x_ref, g_ref, w_ref, o_ref, *, scale):
    xn = _rms(x_ref[...], g_ref[...]).astype(BF16)
    o_ref[...] = (_mm(xn, w_ref[...]) * scale).astype(o_ref.dtype)


def _norm_proj(h, g, w, scale, name):
    t, d = h.shape
    n = w.shape[1]
    tm = PROJ_TM
    return pl.pallas_call(
        functools.partial(_norm_proj_kernel, scale=scale),
        out_shape=jax.ShapeDtypeStruct((t, n), BF16),
        grid=(t // tm,),
        in_specs=[pl.BlockSpec((tm, d), lambda i: (i, 0)), _const_spec((1, d)),
                  _const_spec((d, n))],
        out_specs=pl.BlockSpec((tm, n), lambda i: (i, 0)),
        compiler_params=_params(("arbitrary",)),
        name=name,
    )(h, g.reshape(1, d), w.astype(BF16))


def _qkv_proj_kernel(x_ref, g_ref, wk_ref, wvt_ref, gq_ref, wq_ref, k_ref, vt_ref, q_ref, *, scale):
    x = x_ref[...]
    xhat = x * lax.rsqrt(jnp.mean(x * x, axis=-1, keepdims=True) + EPS)
    xn = (xhat * g_ref[...]).astype(BF16)
    k_ref[...] = _mm(xn, wk_ref[...]).astype(k_ref.dtype)
    vt_ref[...] = _mm_nt(wvt_ref[...], xn).astype(vt_ref.dtype)
    q_ref[...] = (_mm((xhat * gq_ref[...]).astype(BF16), wq_ref[...]) * scale).astype(q_ref.dtype)


def _qkv_proj(h, g, w_kv, gq, w_q, scale):
    t, d = h.shape
    width = w_kv.shape[1] // 2
    tm = PROJ_TM
    return pl.pallas_call(
        functools.partial(_qkv_proj_kernel, scale=scale),
        out_shape=(jax.ShapeDtypeStruct((t, width), BF16), jax.ShapeDtypeStruct((width, t), BF16),
                   jax.ShapeDtypeStruct((t, w_q.shape[1]), BF16)),
        grid=(t // tm,),
        in_specs=[pl.BlockSpec((tm, d), lambda i: (i, 0)), _const_spec((1, d)),
                  _const_spec((d, width)), _const_spec((width, d)),
                  _const_spec((1, d)), _const_spec((d, w_q.shape[1]))],
        out_specs=(pl.BlockSpec((tm, width), lambda i: (i, 0)),
                   pl.BlockSpec((width, tm), lambda i: (0, i)),
                   pl.BlockSpec((tm, w_q.shape[1]), lambda i: (i, 0))),
        compiler_params=_params(("arbitrary",)),
        name="qkv_proj",
    )(h, g.reshape(1, d), w_kv[:, :width].astype(BF16), w_kv[:, width:].T.astype(BF16),
      gq.reshape(1, d), w_q.astype(BF16))


ATT_PAIR = 2 * CHUNK
ATT_BAND = (LEFT_CHUNKS + 2) * CHUNK
ATT_HIST = LEFT_CHUNKS * CHUNK
ATT_SLOTS = 8


def _attn_kernel(q_ref, kp_ref, kc_ref, vp_ref, vc_ref, b_ref, o_ref,
                 kbuf, vbuf, bhist, s_ref, e_ref, *, qb):
    i = pl.program_id(2)
    kbuf[0:ATT_HIST, :] = kp_ref[...]
    kbuf[ATT_HIST:, :] = kc_ref[...]
    vbuf[:, 0:ATT_HIST] = vp_ref[...]
    vbuf[:, ATT_HIST:] = vc_ref[...]
    left = lax.broadcasted_iota(jnp.int32, (ATT_PAIR, 128), 1) < B_HEAD_DIM
    zero = jnp.zeros((), BF16)
    nhist = ATT_HIST // ATT_PAIR
    hd = B_HEAD_DIM

    @pl.when(i == 0)
    def _():
        key = lax.broadcasted_iota(jnp.int32, (ATT_BAND, 1), 0)
        for p in range(nhist):
            bhist[p] = b_ref[0] + jnp.where(key < ATT_HIST - p * ATT_PAIR, NEG_INF, 0.0)

    @pl.when(i == 1)
    def _():
        for p in range(nhist):
            bhist[p] = b_ref[0]

    def scores(p):
        r0 = p * ATT_PAIR
        qp = q_ref[r0:r0 + ATT_PAIR, :]
        qq = jnp.concatenate([jnp.where(left, qp, zero), jnp.where(left, zero, qp)], axis=0)
        s_ref[p % ATT_SLOTS] = _mm_nt(kbuf[r0:r0 + ATT_BAND, :], qq)

    npair = qb // ATT_PAIR
    ahead = ATT_SLOTS - 1
    for p in range(min(ahead, npair)):
        scores(p)
    for p in range(npair):
        r0 = p * ATT_PAIR
        if p + ahead < npair:
            scores(p + ahead)
        b = bhist[p] if p < nhist else b_ref[0]
        m = jnp.max(s_ref[p % ATT_SLOTS] + b, axis=0, keepdims=True)
        e = jnp.exp2((s_ref[p % ATT_SLOTS] - m) + b)
        l = jnp.sum(e, axis=0, keepdims=True)
        e_ref[p % ATT_SLOTS] = e.astype(BF16)
        ot = _mm(vbuf[:, r0:r0 + ATT_BAND], e_ref[p % ATT_SLOTS]) / l
        ot = jnp.concatenate([ot[0:hd, 0:ATT_PAIR], ot[hd:2 * hd, ATT_PAIR:]], axis=0)
        o_ref[r0:r0 + ATT_PAIR, :] = ot.T.astype(o_ref.dtype)


def _band_bias(rel_bias):
    c = CHUNK
    band = (LEFT_CHUNKS + 1) * c
    nh = rel_bias.shape[0]
    assert c - 1 <= REL_CLIP < band - 1
    ramp = rel_bias[:, REL_CLIP - (c - 1):]
    g = jnp.concatenate([ramp, jnp.broadcast_to(rel_bias[:, -1:], (nh, band - 1 - REL_CLIP))], axis=1)
    period = band + c
    u = jnp.pad(g, ((0, 0), (0, period - g.shape[1])))
    hk = jnp.tile(u, (1, c + 1))[:, :c * (period + 1)].reshape(nh, c, period + 1)[:, :, :band]
    bias = hk[:, :, ::-1].astype(F32) * LOG2E
    top = jnp.pad(bias, ((0, 0), (0, 0), (0, CHUNK)), constant_values=NEG_INF)
    bot = jnp.pad(bias, ((0, 0), (0, 0), (CHUNK, 0)), constant_values=NEG_INF)
    return jnp.concatenate([top, bot], axis=1)


def _band_attn(q, k, vt, rel_bias, bsz, seq):
    t, width = q.shape
    nhp = width // 128
    qb = ATT_QB
    nq = seq // qb
    assert nq >= 2 and qb % ATT_HIST == 0
    hpb = qb // ATT_HIST
    bias_t = _band_bias(rel_bias).reshape(nhp, 2 * ATT_PAIR, ATT_BAND).transpose(0, 2, 1)

    def cur(b, hp, i):
        return (b * nq + i, hp)

    def prev(b, hp, i):
        return (jnp.maximum((b * nq + i) * hpb - 1, 0), hp)

    def vcur(b, hp, i):
        return (hp, b * nq + i)

    def vprev(b, hp, i):
        return (hp, jnp.maximum((b * nq + i) * hpb - 1, 0))

    return pl.pallas_call(
        functools.partial(_attn_kernel, qb=qb),
        out_shape=jax.ShapeDtypeStruct((t, width), BF16),
        grid=(bsz, nhp, nq),
        in_specs=[pl.BlockSpec((qb, 128), cur),
                  pl.BlockSpec((ATT_HIST, 128), prev), pl.BlockSpec((qb, 128), cur),
                  pl.BlockSpec((128, ATT_HIST), vprev), pl.BlockSpec((128, qb), vcur),
                  pl.BlockSpec((1, ATT_BAND, 2 * ATT_PAIR), lambda b, hp, i: (hp, 0, 0))],
        out_specs=pl.BlockSpec((qb, 128), cur),
        scratch_shapes=[pltpu.VMEM((ATT_HIST + qb, 128), BF16), pltpu.VMEM((128, ATT_HIST + qb), BF16),
                        pltpu.VMEM((ATT_HIST // ATT_PAIR, ATT_BAND, 2 * ATT_PAIR), F32),
                        pltpu.VMEM((ATT_SLOTS, ATT_BAND, 2 * ATT_PAIR), F32),
                        pltpu.VMEM((ATT_SLOTS, ATT_BAND, 2 * ATT_PAIR), BF16)],
        compiler_params=_params(("arbitrary", "arbitrary", "arbitrary")),
        name="band_attn",
    )(q, k, k, vt, vt, bias_t)


def _lane_col(x, idx):
    lane = lax.broadcasted_iota(jnp.int32, x.shape, 1)
    return jnp.sum(jnp.where(lane == idx, x, 0.0), axis=-1, keepdims=True)


def _wy_prep(q, k, v, beta, gc, eg, kef):
    c = CHUNK
    rows = q.shape[0]
    n = rows // c
    kb = k * beta
    qd = q * eg
    ke = k * kef
    rhs = jnp.concatenate([v * beta, kb * eg], axis=1).astype(BF16)
    kb16 = kb.astype(BF16)
    q16 = q.astype(BF16)
    k16 = k.astype(BF16)
    row = lax.broadcasted_iota(jnp.int32, (c, 128), 0)
    lane = lax.broadcasted_iota(jnp.int32, (c, 128), 1)
    col = lane & (c - 1)
    left = lane < c
    causal = row >= col
    strict = row > col
    eye_r = jnp.where(lane == row + c, 1.0, 0.0)
    sl = [slice(i * c, (i + 1) * c) for i in range(n)]
    sc = [_mm_nt(jnp.concatenate([kb16[s], q16[s]], axis=0),
                 jnp.concatenate([k16[s], k16[s]], axis=0)) for s in sl]
    ws, a_out = [], []
    for i, s in enumerate(sl):
        gci = jnp.broadcast_to(gc[s], (c, 128))
        gcol = jnp.sum(jnp.where(row == col, gci, 0.0), axis=0, keepdims=True)
        decay = jnp.where(causal, jnp.exp(jnp.where(causal, gci - gcol, 0.0)), 0.0)
        m = jnp.where(strict, sc[i][:c] * decay, 0.0)
        a_out.append((sc[i][c:] * decay)[:, :c])
        ws.append(jnp.where(left, -m, eye_r))
    for _ in range(6):
        rs = [_mm(w[:, :c].astype(BF16), w.astype(BF16)) for w in ws]
        ws = [r + jnp.where(left, 0.0, w) for r, w in zip(rs, ws)]
    uw = [_mm(w[:, c:].astype(BF16), rhs[s]) for w, s in zip(ws, sl)]
    return uw, a_out, qd, ke


def _a_pre_kernel(x_ref, halo_ref, g_ref, w_ref, wg_ref, cw_ref, hp_ref,
                  wq_ref, ke_ref, u_ref, a_ref, dec_ref, z_ref,
                  xn_ref, pj_ref, *, tm, tiles_per_seq):
    _fill_xn(xn_ref, x_ref[...], halo_ref[...], g_ref[...], pl.program_id(0), tiles_per_seq)
    nh, hd, c = A_HEADS, A_HEAD_DIM, CHUNK
    n = tm // c
    qk_scale = hd ** -0.5

    gr = _mm(xn_ref[HALO:, :], wg_ref[...])
    hp = hp_ref[...]
    beta_all = jax.nn.sigmoid(gr)
    a_in = gr + hp[1:2]
    g_all = -jnp.exp(hp[0:1]) * (jnp.maximum(a_in, 0.0) + jnp.log1p(jnp.exp(-jnp.abs(a_in))))
    rin = lax.broadcasted_iota(jnp.int32, (tm, 128), 0) & (c - 1)
    gc_all = g_all
    for s in (1, 2, 4, 8, 16, 32):
        gc_all = gc_all + jnp.where(rin >= s, pltpu.roll(gc_all, s, axis=0), 0.0)
    g_last = [gc_all[(i + 1) * c - 1:(i + 1) * c, :] for i in range(n)]
    eg_all = jnp.exp(gc_all)
    kef_all = jnp.exp(jnp.concatenate([jnp.broadcast_to(gl, (c, 128)) for gl in g_last], axis=0) - gc_all)
    dec_all = jnp.exp(jnp.concatenate(g_last, axis=0))

    def project(h):
        pj_ref[h % 2] = _mm(xn_ref[...], w_ref[h])

    project(0)
    for h in range(nh):
        if h + 1 < nh:
            project(h + 1)
        pj = pj_ref.at[h % 2]
        cw = cw_ref[h]
        pall = pj[:, 0:3 * hd]
        hor = cw[0:1] * pall
        for tap in (1, 2, 3):
            hor = pltpu.roll(hor, 1, axis=0) + cw[tap:tap + 1] * pall
        qkv = hor[HALO:, :]
        qkv = qkv * jax.nn.sigmoid(qkv)
        q = qkv[:, 0:hd]
        k = qkv[:, hd:2 * hd]
        v = qkv[:, 2 * hd:3 * hd]
        q = q * (lax.rsqrt(jnp.sum(q * q, axis=-1, keepdims=True) + EPS) * qk_scale)
        k = k * lax.rsqrt(jnp.sum(k * k, axis=-1, keepdims=True) + EPS)
        z_ref[h] = pj[pl.ds(HALO, tm), 3 * hd:4 * hd]
        uw, a_out, qd, ke = _wy_prep(q, k, v, _lane_col(beta_all, h), _lane_col(gc_all, nh + h),
                                     _lane_col(eg_all, nh + h), _lane_col(kef_all, nh + h))
        for i in range(n):
            r0 = i * c
            wq_ref[h, 2 * r0:2 * r0 + c, :] = uw[i][:, hd:].astype(BF16)
            wq_ref[h, 2 * r0 + c:2 * r0 + 2 * c, :] = qd[r0:r0 + c].astype(BF16)
            u_ref[h, r0:r0 + c, :] = uw[i][:, :hd]
            a_ref[h, r0:r0 + c, :] = a_out[i].astype(BF16)
        ke_ref[h] = ke.astype(BF16)
        dec_ref[0, h] = jnp.broadcast_to(_lane_col(dec_all, nh + h), (n, hd))


def _a_pre(h, g, w_in, conv_w, a_log, dt_bias, seq):
    t, d = h.shape
    nh, hd = A_HEADS, A_HEAD_DIM
    width = nh * hd
    tm = APRE_TM
    wcat = jnp.concatenate([w_in[:, s * width:(s + 1) * width].reshape(d, nh, hd) for s in range(4)],
                           axis=2).transpose(1, 0, 2).astype(BF16)
    wgate = jnp.pad(w_in[:, 4 * width:], ((0, 0), (0, 128 - 2 * nh))).astype(BF16)
    cw = jnp.concatenate([conv_w[:, s * width:(s + 1) * width].reshape(-1, nh, hd)
                          for s in range(3)], axis=2).transpose(1, 0, 2)
    hp = jnp.pad(jnp.stack([a_log, dt_bias], axis=0), ((0, 0), (nh, 128 - 2 * nh)))
    kern = functools.partial(_a_pre_kernel, tm=tm, tiles_per_seq=seq // tm)
    out_shape = (
        jax.ShapeDtypeStruct((nh, 2 * t, hd), BF16),
        jax.ShapeDtypeStruct((nh, t, hd), BF16),
        jax.ShapeDtypeStruct((nh, t, hd), F32),
        jax.ShapeDtypeStruct((nh, t, CHUNK), BF16),
        jax.ShapeDtypeStruct((t // tm, nh, DEC_ROWS, hd), F32),
        jax.ShapeDtypeStruct((nh, t, hd), F32),
    )
    return pl.pallas_call(
        kern,
        out_shape=out_shape,
        grid=(t // tm,),
        in_specs=[pl.BlockSpec((tm, d), lambda i: (i, 0)), _halo_spec(tm, d),
                  _const_spec((1, d)), _const_spec((nh, d, 4 * hd)), _const_spec((d, 128)),
                  _const_spec((nh, 4, 3 * hd)), _const_spec((2, 128))],
        out_specs=(
            pl.BlockSpec((nh, 2 * tm, hd), lambda i: (0, i, 0)),
            pl.BlockSpec((nh, tm, hd), lambda i: (0, i, 0)),
            pl.BlockSpec((nh, tm, hd), lambda i: (0, i, 0)),
            pl.BlockSpec((nh, tm, CHUNK), lambda i: (0, i, 0)),
            pl.BlockSpec((1, nh, DEC_ROWS, hd), lambda i: (i, 0, 0, 0)),
            pl.BlockSpec((nh, tm, hd), lambda i: (0, i, 0)),
        ),
        scratch_shapes=[pltpu.VMEM((tm + HALO, d), BF16), pltpu.VMEM((2, tm + HALO, 4 * hd), F32)],
        compiler_params=_params(("arbitrary",)),
        name="a_pre",
    )(h, h, g.reshape(1, d), wcat, wgate, cw, hp)


def _a_rec_kernel(wq_ref, ke_ref, u_ref, a_ref, dec_ref, z_ref, onw_ref,
                  y_ref, state_ref, o_ref, *, cb, nb):
    nh, hd = A_HEADS, A_HEAD_DIM
    rows = cb * CHUNK
    j = pl.program_id(1)

    @pl.when(j == 0)
    def _():
        state_ref[...] = jnp.zeros_like(state_ref)

    first = (j % (APRE_NCT // cb)) * cb
    chains = [(b, h) for b in range(nb) for h in range(nh)]

    def chunk(c, carry):
        r0 = pl.multiple_of(c * CHUNK, CHUNK)
        r1 = pl.multiple_of(c * 2 * CHUNK, 2 * CHUNK)
        st = [state_ref[b * nh + h] for b, h in chains]
        ws = [_mm(wq_ref[h, b, pl.ds(r1, 2 * CHUNK), :], s.astype(BF16))
              for (b, h), s in zip(chains, st)]
        v_new = [(u_ref[h, b, pl.ds(r0, CHUNK), :] - w[:CHUNK]).astype(BF16) for (b, h), w in zip(chains, ws)]
        av = [_mm(a_ref[h, b, pl.ds(r0, CHUNK), :], v) for (b, h), v in zip(chains, v_new)]
        upd = [_mm_tn(ke_ref[h, b, pl.ds(r0, CHUNK), :], v) for (b, h), v in zip(chains, v_new)]
        for i, (b, h) in enumerate(chains):
            o_ref[b, h, pl.ds(r0, CHUNK), :] = ws[i][CHUNK:] + av[i]
            dec = dec_ref[b, 0, h, pl.ds(first + c, 1), :]
            state_ref[b * nh + h] = st[i] * dec + upd[i]
        return carry

    lax.fori_loop(0, cb, chunk, 0)
    onw = onw_ref[...]
    for b in range(nb):
        for h in range(nh):
            z = z_ref[h, b]
            y_ref[b, :, h * hd:(h + 1) * hd] = (_rms(o_ref[b, h], onw) * (z * jax.nn.sigmoid(z))).astype(BF16)


def _a_rec(pre, out_norm_w, bsz, seq):
    wq, ke, u, a, dec, z = pre
    nh, hd = A_HEADS, A_HEAD_DIM
    cb, nb = AREC_CB, AREC_NB
    rows = cb * CHUNK
    nblk = seq // rows
    assert APRE_NCT % cb == 0 and bsz % nb == 0

    def by_seq(x, rows_per_seq):
        return x.reshape(nh, bsz, rows_per_seq, x.shape[-1])

    def idx(g, j):
        return (0, g, j, 0)

    y = pl.pallas_call(
        functools.partial(_a_rec_kernel, cb=cb, nb=nb),
        out_shape=jax.ShapeDtypeStruct((bsz, seq, nh * hd), BF16),
        grid=(bsz // nb, nblk),
        in_specs=[pl.BlockSpec((nh, nb, 2 * rows, hd), idx), pl.BlockSpec((nh, nb, rows, hd), idx),
                  pl.BlockSpec((nh, nb, rows, hd), idx), pl.BlockSpec((nh, nb, rows, CHUNK), idx),
                  pl.BlockSpec((nb, 1, nh, DEC_ROWS, hd), lambda g, j: (g, j * cb // APRE_NCT, 0, 0, 0)),
                  pl.BlockSpec((nh, nb, rows, hd), idx),
                  _const_spec((1, hd))],
        out_specs=pl.BlockSpec((nb, rows, nh * hd), lambda g, j: (g, j, 0)),
        scratch_shapes=[pltpu.VMEM((nb * nh, hd, hd), F32), pltpu.VMEM((nb, nh, rows, hd), F32)],
        compiler_params=_params(("arbitrary", "arbitrary")),
        name="a_rec",
    )(by_seq(wq, 2 * seq), by_seq(ke, seq), by_seq(u, seq), by_seq(a, seq),
      dec.reshape(bsz, seq // APRE_TM, nh, DEC_ROWS, hd), by_seq(z, seq), out_norm_w.reshape(1, hd))
    return y.reshape(bsz * seq, nh * hd)


def kernel(x, a_norm, a_w_in, a_conv, a_A_log, a_dt_bias, a_out_norm, a_w_out, kv_norm, w_kv, b_norm, b_w_q, b_rel_bias, b_w_out, f_norm, f_w_up, f_conv, f_conv_b, f_w_down, final_norm):
    bsz, seq, d = x.shape
    n_a = a_norm.shape[0]
    n_b = b_norm.shape[0]
    depth = n_a + n_b
    h = x.reshape(bsz * seq, d)
    kv = None
    for layer in range(depth):
        if layer < n_a:
            i = layer
            pre = _a_pre(h, a_norm[i], a_w_in[i], a_conv[i], a_A_log[i], a_dt_bias[i], seq)
            y, w_o = _a_rec(pre, a_out_norm[i], bsz, seq), a_w_out[i]
        else:
            j = layer - n_a
            q_scale = B_HEAD_DIM ** -0.5 * LOG2E
            if kv is None:
                k, vt, q = _qkv_proj(h, kv_norm, w_kv, b_norm[j], b_w_q[j], q_scale)
                kv = (k, vt)
            else:
                q = _norm_proj(h, b_norm[j], b_w_q[j], q_scale, "q_proj")
            y, w_o = _band_attn(q, kv[0], kv[1], b_rel_bias[j], bsz, seq), b_w_out[j]
        fg = final_norm if layer == depth - 1 else None
        h = _ffn(h, y, w_o, f_norm[layer], f_w_up[layer], f_conv[layer], f_conv_b[layer],
                 f_w_down[layer], fg, seq)
    return h.reshape(bsz, seq, d)
```
